```python
import jax, jax.numpy as jnp
from jax import lax
import numpy as np

D_MODEL = 1024
BATCH = 8
SEQ = 2048
DEPTH = 4

N_MIXERS = 2
CONV_WIDTH = 3
ML_HEADS = 4
ML_V_DIM = D_MODEL // ML_HEADS
ML_QK_DIM = ML_V_DIM // 2
ML_QK_W = ML_HEADS * ML_QK_DIM
ML_V_W = ML_HEADS * ML_V_DIM
ML_IN_W = 2 * ML_QK_W + 2 * ML_V_W + 2 * ML_HEADS
ML_CHUNK = 64
GATE_SOFTCAP = 15.0
D_FF = ((8 * D_MODEL // 3 + 127) // 128) * 128
EPS = 1e-6

kernel_name = "hybrid_shortconv_mlstm_convffn_adaln"


def rms_norm(x, g):
    xf = x.astype(jnp.float32)
    y = xf * lax.rsqrt(jnp.mean(xf * xf, axis=-1, keepdims=True) + EPS)
    return (y * g.astype(jnp.float32)).astype(x.dtype)


def modulate(h, shift, scale):
    return h * (1 + scale[:, None, :]) + shift[:, None, :]


def causal_dwconv(x, w):
    k_w = w.shape[0]
    s = x.shape[1]
    xp = jnp.pad(x, ((0, 0), (k_w - 1, 0), (0, 0)))
    y = w[0] * xp[:, 0:s]
    for j in range(1, k_w):
        y = y + w[j] * xp[:, j:j + s]
    return y


def softcap(t):
    return GATE_SOFTCAP * jnp.tanh(t / GATE_SOFTCAP)


def short_conv_mixer(h, w_in, conv_w, w_out):
    bcu = h @ w_in
    gb, gc, u = jnp.split(bcu, 3, axis=-1)
    y = gb * causal_dwconv(gc * u, conv_w)
    return y @ w_out


def mlstm_chunkwise(q, k, v, ig, logf):
    bsz, nh, s, dk = q.shape
    dv = v.shape[-1]
    nc = s // ML_CHUNK

    def to_chunks(t):
        t = t.reshape((bsz, nh, nc, ML_CHUNK) + t.shape[3:])
        return jnp.moveaxis(t, 2, 0)

    causal = jnp.tril(jnp.ones((ML_CHUNK, ML_CHUNK), dtype=bool))

    def step(carry, inp):
        c_st, n_st, m_st = carry
        qc, kc, vc, ic, fc = inp
        b = jnp.cumsum(fc, axis=-1)
        d_log = b[..., :, None] - b[..., None, :] + ic[..., None, :]
        d_log = jnp.where(causal, d_log, -jnp.inf)
        inter = b + m_st[..., None]
        m_t = jnp.maximum(inter, jnp.max(d_log, axis=-1))
        p = jnp.exp(d_log - m_t[..., None])
        sc = jnp.einsum('bhtd,bhsd->bhts', qc, kc) * p
        a = jnp.exp(inter - m_t)
        num = a[..., None] * jnp.einsum('bhvd,bhtd->bhtv', c_st, qc) + jnp.einsum('bhts,bhsv->bhtv', sc, vc)
        den = a * jnp.einsum('bhd,bhtd->bht', n_st, qc) + jnp.sum(sc, axis=-1)
        hc = num / jnp.maximum(jnp.abs(den), jnp.exp(-m_t))[..., None]
        b_last = b[..., -1]
        w_log = b_last[..., None] - b + ic
        m_new = jnp.maximum(b_last + m_st, jnp.max(w_log, axis=-1))
        decay = jnp.exp(b_last + m_st - m_new)
        w = jnp.exp(w_log - m_new[..., None])
        c_new = decay[..., None, None] * c_st + jnp.einsum('bhs,bhsv,bhsd->bhvd', w, vc, kc)
        n_new = decay[..., None] * n_st + jnp.einsum('bhs,bhsd->bhd', w, kc)
        return (c_new, n_new, m_new), hc

    init = (jnp.zeros((bsz, nh, dv, dk), jnp.float32),
            jnp.zeros((bsz, nh, dk), jnp.float32),
            jnp.zeros((bsz, nh), jnp.float32))
    _, hs = lax.scan(step, init, (to_chunks(q), to_chunks(k), to_chunks(v), to_chunks(ig), to_chunks(logf)))
    return jnp.moveaxis(hs, 0, 2).reshape(bsz, nh, s, dv)


def mlstm_mixer(h, w_in, b_i, b_f, norm_g, w_out):
    bsz, s, _ = h.shape
    proj = h @ w_in
    q, k, v, o, ig, fg = jnp.split(
        proj, [ML_QK_W, 2 * ML_QK_W, 2 * ML_QK_W + ML_V_W, 2 * ML_QK_W + 2 * ML_V_W,
               2 * ML_QK_W + 2 * ML_V_W + ML_HEADS], axis=-1)

    def heads(t, d):
        return t.reshape(bsz, s, ML_HEADS, d).transpose(0, 2, 1, 3).astype(jnp.float32)

    qh = heads(q, ML_QK_DIM) * (ML_QK_DIM ** -0.5)
    kh = heads(k, ML_QK_DIM)
    vh = heads(v, ML_V_DIM)
    i_pre = softcap((ig + b_i).astype(jnp.float32)).transpose(0, 2, 1)
    logf = jax.nn.log_sigmoid(softcap((fg + b_f).astype(jnp.float32))).transpose(0, 2, 1)
    hh = mlstm_chunkwise(qh, kh, vh, i_pre, logf)
    hh = hh * lax.rsqrt(jnp.mean(hh * hh, axis=-1, keepdims=True) + EPS)
    hh = hh.transpose(0, 2, 1, 3).reshape(bsz, s, ML_V_W) * norm_g.astype(jnp.float32)
    y = (hh * jax.nn.sigmoid(o.astype(jnp.float32))).astype(h.dtype)
    return y @ w_out


def conv_ffn(h, w_up, conv_w, conv_b, w_down):
    u = causal_dwconv(h @ w_up, conv_w) + conv_b
    g, val = jnp.split(u, 2, axis=-1)
    return (jax.nn.silu(g) * val) @ w_down


def setup_inputs(seed: int = 0) -> dict:
    key = jax.random.key(seed)
    ks = jax.random.split(key, 24)
    n_sc = (DEPTH + 1) // 2
    n_ml = DEPTH // 2
    d = D_MODEL
    f32 = jnp.float32
    nrm = lambda k, shape, s: jax.random.normal(k, shape, f32) * s
    return {
        "x": nrm(ks[0], (BATCH, SEQ, d), 1.0),
        "c": nrm(ks[1], (BATCH, d), 1.0),
        "ada_w": nrm(ks[2], (DEPTH, d, 6 * d), 0.2 * d ** -0.5),
        "ada_b": nrm(ks[3], (DEPTH, 6 * d), 0.02),
        "norm_mix_g": 1.0 + nrm(ks[4], (DEPTH, d), 0.05),
        "norm_ffn_g": 1.0 + nrm(ks[5], (DEPTH, d), 0.05),
        "sc_w_in": nrm(ks[6], (n_sc, d, 3 * d), d ** -0.5),
        "sc_conv_w": nrm(ks[7], (n_sc, CONV_WIDTH, d), CONV_WIDTH ** -0.5),
        "sc_w_out": nrm(ks[8], (n_sc, d, d), d ** -0.5),
        "ml_w_in": nrm(ks[9], (n_ml, d, ML_IN_W), d ** -0.5),
        "ml_b_i": nrm(ks[10], (n_ml, ML_HEADS), 0.1),
        "ml_b_f": jnp.linspace(3.0, 6.0, ML_HEADS, dtype=f32)[None, :] + nrm(ks[11], (n_ml, ML_HEADS), 0.1),
        "ml_norm_g": 1.0 + nrm(ks[12], (n_ml, ML_V_W), 0.05),
        "ml_w_out": nrm(ks[13], (n_ml, ML_V_W, d), ML_V_W ** -0.5),
        "ffn_w_up": nrm(ks[14], (DEPTH, d, 2 * D_FF), d ** -0.5),
        "ffn_conv_w": nrm(ks[15], (DEPTH, CONV_WIDTH, 2 * D_FF), CONV_WIDTH ** -0.5),
        "ffn_conv_b": nrm(ks[16], (DEPTH, 2 * D_FF), 0.02),
        "ffn_w_down": nrm(ks[17], (DEPTH, D_FF, d), D_FF ** -0.5),
        "final_norm_g": 1.0 + nrm(ks[18], (d,), 0.05),
    }


def reference(x, c, ada_w, ada_b, norm_mix_g, norm_ffn_g, sc_w_in, sc_conv_w, sc_w_out,
              ml_w_in, ml_b_i, ml_b_f, ml_norm_g, ml_w_out, ffn_w_up, ffn_conv_w, ffn_conv_b,
              ffn_w_down, final_norm_g):
    cond = jax.nn.silu(c)
    for layer in range(DEPTH):
        mod = cond @ ada_w[layer] + ada_b[layer]
        sh1, sc1, g1, sh2, sc2, g2 = jnp.split(mod, 6, axis=-1)
        h = modulate(rms_norm(x, norm_mix_g[layer]), sh1, sc1)
        j = layer // N_MIXERS
        if layer % N_MIXERS == 0:
            y = short_conv_mixer(h, sc_w_in[j], sc_conv_w[j], sc_w_out[j])
        else:
            y = mlstm_mixer(h, ml_w_in[j], ml_b_i[j], ml_b_f[j], ml_norm_g[j], ml_w_out[j])
        x = x + (1 + g1)[:, None, :] * y
        h = modulate(rms_norm(x, norm_ffn_g[layer]), sh2, sc2)
        x = x + (1 + g2)[:, None, :] * conv_ffn(h, ffn_w_up[layer], ffn_conv_w[layer], ffn_conv_b[layer], ffn_w_down[layer])
    return rms_norm(x, final_norm_g)
```

```python
import functools

import jax
import jax.numpy as jnp
from jax import lax
from jax.experimental import pallas as pl
from jax.experimental.pallas import tpu as pltpu

D_MODEL = 1024
DEPTH = 4
CONV_WIDTH = 3
ML_HEADS = 4
ML_V_DIM = D_MODEL // ML_HEADS
ML_QK_DIM = ML_V_DIM // 2
ML_QK_W = ML_HEADS * ML_QK_DIM
ML_V_W = ML_HEADS * ML_V_DIM
GATE_SOFTCAP = 15.0
D_FF = 2816
EPS = 1e-6

V7X_LANES = 128
V7X_SUBLANES = 8
V7X_MXU_DIM = 256
V7X_VMEM_BYTES = 64 * 1024 * 1024
VMEM_LIMIT_BYTES = V7X_VMEM_BYTES - 8 * 1024 * 1024

SEQ_TILE = 256
FF_CHUNK = V7X_MXU_DIM
SC_CHUNK = V7X_MXU_DIM
ADA_TILE = 1536

BF16 = jnp.bfloat16
F32 = jnp.float32


def _dot(a, b):
    return jnp.dot(a, b, preferred_element_type=F32)


def _norm_modulate(x, g, shift, scale):
    ms = jnp.mean(x * x, axis=-1, keepdims=True)
    y = x * lax.rsqrt(ms + EPS)
    return (y * g) * (1.0 + scale) + shift


def _causal_conv3(p, prev, w):
    t = p.shape[0]
    ext = jnp.concatenate([prev, p], axis=0)
    p1 = ext[V7X_SUBLANES - 1:V7X_SUBLANES - 1 + t]
    p2 = ext[V7X_SUBLANES - 2:V7X_SUBLANES - 2 + t]
    return w[0:1] * p2 + w[1:2] * p1 + w[2:3] * p


def _resident(block_shape, index_map):
    return pl.BlockSpec(block_shape, index_map, pipeline_mode=pl.Buffered(1))


def _compiler_params():
    return pltpu.CompilerParams(
        dimension_semantics=("arbitrary", "arbitrary"),
        vmem_limit_bytes=VMEM_LIMIT_BYTES)


def _ada_kernel(c_ref, w_ref, b_ref, o_ref):
    c = c_ref[...]
    cond = (c * jax.nn.sigmoid(c)).astype(BF16)
    o_ref[...] = _dot(cond, w_ref[...].astype(BF16)) + b_ref[...]


def _ada_modulation(c, ada_w, ada_b):
    bsz, d = c.shape
    n = ada_w.shape[-1]
    out = pl.pallas_call(
        _ada_kernel,
        grid=(DEPTH, n // ADA_TILE),
        in_specs=[
            pl.BlockSpec((bsz, d), lambda l, j: (0, 0)),
            pl.BlockSpec((None, d, ADA_TILE), lambda l, j: (l, 0, j)),
            pl.BlockSpec((None, 1, ADA_TILE), lambda l, j: (l, 0, j)),
        ],
        out_specs=pl.BlockSpec((None, bsz, ADA_TILE), lambda l, j: (l, 0, j)),
        out_shape=jax.ShapeDtypeStruct((DEPTH, bsz, n), F32),
        compiler_params=_compiler_params(),
        name="adaln_mod",
    )(c, ada_w, ada_b.reshape(DEPTH, 1, n))
    return out.reshape(DEPTH, bsz, 6, d)


def _sconv_kernel(x_ref, mod_ref, g_ref, win_ref, cw_ref, wout_ref, o_ref,
                  tail_ref, y_ref):
    @pl.when(pl.program_id(1) == 0)
    def _():
        tail_ref[...] = jnp.zeros_like(tail_ref)

    x = x_ref[...]
    t = x.shape[0]
    mod = mod_ref[...]
    h = _norm_modulate(x, g_ref[...], mod[0:1], mod[1:2]).astype(BF16)
    for j in range(D_MODEL // SC_CHUNK):
        lo = j * SC_CHUNK
        gb = _dot(h, win_ref[:, lo:lo + SC_CHUNK])
        gc = _dot(h, win_ref[:, D_MODEL + lo:D_MODEL + lo + SC_CHUNK])
        u = _dot(h, win_ref[:, 2 * D_MODEL + lo:2 * D_MODEL + lo + SC_CHUNK])
        v = gc * u
        prev = tail_ref[:, lo:lo + SC_CHUNK]
        tail_ref[:, lo:lo + SC_CHUNK] = v[t - V7X_SUBLANES:]
        conv = _causal_conv3(v, prev, cw_ref[:, lo:lo + SC_CHUNK])
        y_ref[:, lo:lo + SC_CHUNK] = (gb * conv).astype(BF16)
    out = _dot(y_ref[...], wout_ref[...])
    o_ref[...] = x + (1.0 + mod[2:3]) * out


def _sconv_layer(x, mods, layer, norm_g, w_in, conv_w, w_out, j):
    bsz, s, d = x.shape
    tm = SEQ_TILE
    return pl.pallas_call(
        _sconv_kernel,
        grid=(bsz, s // tm),
        in_specs=[
            pl.BlockSpec((None, tm, d), lambda b, i: (b, i, 0)),
            pl.BlockSpec((None, None, 6, d), lambda b, i: (layer, b, 0, 0)),
            pl.BlockSpec((None, 1, d), lambda b, i: (layer, 0, 0)),
            _resident((None, d, 3 * d), lambda b, i: (j, 0, 0)),
            pl.BlockSpec((None, CONV_WIDTH, d), lambda b, i: (j, 0, 0)),
            _resident((None, d, d), lambda b, i: (j, 0, 0)),
        ],
        out_specs=pl.BlockSpec((None, tm, d), lambda b, i: (b, i, 0)),
        out_shape=jax.ShapeDtypeStruct(x.shape, F32),
        scratch_shapes=[
            pltpu.VMEM((V7X_SUBLANES, d), F32),
            pltpu.VMEM((tm, d), BF16),
        ],
        compiler_params=_compiler_params(),
        name=f"sconv_layer{layer}",
    )(x, mods, norm_g, w_in, conv_w, w_out)


def _ffn_kernel(x_ref, mod_ref, g_ref, wup_ref, cw_ref, cb_ref, wdown_ref,
                fg_ref, o_ref, tail_ref, a_ref, *, final_norm):
    @pl.when(pl.program_id(1) == 0)
    def _():
        tail_ref[...] = jnp.zeros_like(tail_ref)

    x = x_ref[...]
    t = x.shape[0]
    mod = mod_ref[...]
    h = _norm_modulate(x, g_ref[...], mod[3:4], mod[4:5]).astype(BF16)
    for j in range(D_FF // FF_CHUNK):
        halves = []
        for base in (0, D_FF):
            lo = base + j * FF_CHUNK
            p = _dot(h, wup_ref[:, lo:lo + FF_CHUNK])
            prev = tail_ref[:, lo:lo + FF_CHUNK]
            tail_ref[:, lo:lo + FF_CHUNK] = p[t - V7X_SUBLANES:]
            halves.append(_causal_conv3(p, prev, cw_ref[:, lo:lo + FF_CHUNK])
                          + cb_ref[:, lo:lo + FF_CHUNK])
        gate, val = halves
        act = gate * jax.nn.sigmoid(gate) * val
        a_ref[:, j * FF_CHUNK:(j + 1) * FF_CHUNK] = act.astype(BF16)
    out = _dot(a_ref[...], wdown_ref[...])
    y = x + (1.0 + mod[5:6]) * out
    if final_norm:
        ms = jnp.mean(y * y, axis=-1, keepdims=True)
        y = y * lax.rsqrt(ms + EPS) * fg_ref[...]
    o_ref[...] = y


def _ffn_layer(x, mods, layer, norm_g, w_up, conv_w, conv_b, w_down, final_g,
               final_norm):
    bsz, s, d = x.shape
    tm = SEQ_TILE
    f2 = 2 * D_FF
    return pl.pallas_call(
        functools.partial(_ffn_kernel, final_norm=final_norm),
        grid=(bsz, s // tm),
        in_specs=[
            pl.BlockSpec((None, tm, d), lambda b, i: (b, i, 0)),
            pl.BlockSpec((None, None, 6, d), lambda b, i: (layer, b, 0, 0)),
            pl.BlockSpec((None, 1, d), lambda b, i: (layer, 0, 0)),
            _resident((None, d, f2), lambda b, i: (layer, 0, 0)),
            pl.BlockSpec((None, CONV_WIDTH, f2), lambda b, i: (layer, 0, 0)),
            pl.BlockSpec((None, 1, f2), lambda b, i: (layer, 0, 0)),
            _resident((None, D_FF, d), lambda b, i: (layer, 0, 0)),
            pl.BlockSpec((1, d), lambda b, i: (0, 0)),
        ],
        out_specs=pl.BlockSpec((None, tm, d), lambda b, i: (b, i, 0)),
        out_shape=jax.ShapeDtypeStruct(x.shape, F32),
        scratch_shapes=[
            pltpu.VMEM((V7X_SUBLANES, f2), F32),
            pltpu.VMEM((tm, D_FF), BF16),
        ],
        compiler_params=_compiler_params(),
        name=f"ffn_layer{layer}",
    )(x, mods, norm_g, w_up, conv_w, conv_b.reshape(DEPTH, 1, f2), w_down,
      final_g.reshape(1, d))


def _ml_proj_kernel(x_ref, mod_ref, g_ref, win_ref, wg_ref,
                    q_ref, k_ref, v_ref, o_ref, gate_ref):
    x = x_ref[...]
    mod = mod_ref[...]
    h = _norm_modulate(x, g_ref[...], mod[0:1], mod[1:2]).astype(BF16)
    q_ref[...] = (_dot(h, win_ref[:, 0:ML_QK_W])
                  * (ML_QK_DIM ** -0.5)).astype(BF16)
    k_ref[...] = _dot(h, win_ref[:, ML_QK_W:2 * ML_QK_W]).astype(BF16)
    v0 = 2 * ML_QK_W
    v_ref[...] = _dot(h, win_ref[:, v0:v0 + ML_V_W]).astype(BF16)
    o_ref[...] = _dot(h, win_ref[:, v0 + ML_V_W:v0 + 2 * ML_V_W])
    gate_ref[...] = _dot(h, wg_ref[...])


def _ml_proj(x, mods, layer, norm_g, w_main, w_gate, j):
    bsz, s, d = x.shape
    tm = SEQ_TILE
    nmain = w_main.shape[-1]
    tok = lambda b, i: (b, i, 0)
    return pl.pallas_call(
        _ml_proj_kernel,
        grid=(bsz, s // tm),
        in_specs=[
            pl.BlockSpec((None, tm, d), tok),
            pl.BlockSpec((None, None, 6, d), lambda b, i: (layer, b, 0, 0)),
            pl.BlockSpec((None, 1, d), lambda b, i: (layer, 0, 0)),
            _resident((None, d, nmain), lambda b, i: (j, 0, 0)),
            _resident((None, d, V7X_LANES), lambda b, i: (j, 0, 0)),
        ],
        out_specs=[
            pl.BlockSpec((None, tm, ML_QK_W), tok),
            pl.BlockSpec((None, tm, ML_QK_W), tok),
            pl.BlockSpec((None, tm, ML_V_W), tok),
            pl.BlockSpec((None, tm, ML_V_W), tok),
            pl.BlockSpec((None, tm, V7X_LANES), tok),
        ],
        out_shape=[
            jax.ShapeDtypeStruct((bsz, s, ML_QK_W), BF16),
            jax.ShapeDtypeStruct((bsz, s, ML_QK_W), BF16),
            jax.ShapeDtypeStruct((bsz, s, ML_V_W), BF16),
            jax.ShapeDtypeStruct((bsz, s, ML_V_W), F32),
            jax.ShapeDtypeStruct((bsz, s, V7X_LANES), F32),
        ],
        compiler_params=_compiler_params(),
        name=f"mlstm_proj{layer}",
    )(x, mods, norm_g, w_main, w_gate)


def _log_sigmoid(z):
    return jnp.minimum(z, 0.0) - jnp.log1p(jnp.exp(-jnp.abs(z)))


def _ml_cell_kernel(x_ref, mod_ref, q_ref, k_ref, v_ref, o_ref, gate_ref,
                    gbias_ref, ng_ref, wout_ref, out_ref,
                    c_ref, n_ref, m_ref, y_ref):
    @pl.when(pl.program_id(1) == 0)
    def _():
        c_ref[...] = jnp.zeros_like(c_ref)
        n_ref[...] = jnp.zeros_like(n_ref)
        m_ref[...] = jnp.zeros_like(m_ref)

    L = q_ref.shape[0]
    row = lax.broadcasted_iota(jnp.int32, (L, L), 0)
    col = lax.broadcasted_iota(jnp.int32, (L, L), 1)
    causal = row >= col
    tril = jnp.where(causal, 1.0, 0.0).astype(BF16)

    capped = GATE_SOFTCAP * jnp.tanh((gate_ref[...] + gbias_ref[...])
                                     * (1.0 / GATE_SOFTCAP))
    i_log = capped
    f_log = _log_sigmoid(pltpu.roll(capped, V7X_LANES - ML_HEADS, axis=1))
    f_hi = f_log.astype(BF16)
    r1 = f_log - f_hi.astype(F32)
    f_mid = r1.astype(BF16)
    f_lo = (r1 - f_mid.astype(F32)).astype(BF16)
    b = _dot(tril, f_hi) + _dot(tril, f_mid) + _dot(tril, f_lo)
    r = i_log - b
    r_t = r.T
    m_prev = m_ref[...]
    b_last = b[L - 1:L, :]
    inter = b + m_prev
    m_new = b_last + jnp.maximum(m_prev, jnp.max(r, axis=0, keepdims=True))
    decay = jnp.exp(b_last + m_prev - m_new)
    w_all = jnp.exp(b_last + r - m_new)
    m_ref[...] = m_new

    for hd in range(ML_HEADS):
        qh = q_ref[:, hd * ML_QK_DIM:(hd + 1) * ML_QK_DIM]
        kh = k_ref[:, hd * ML_QK_DIM:(hd + 1) * ML_QK_DIM]
        vh = v_ref[:, hd * ML_V_DIM:(hd + 1) * ML_V_DIM]
        d_log = jnp.where(causal, b[:, hd:hd + 1] + r_t[hd:hd + 1, :], -jnp.inf)
        inter_h = inter[:, hd:hd + 1]
        m_t = jnp.maximum(inter_h, jnp.max(d_log, axis=-1, keepdims=True))
        p = jnp.exp(d_log - m_t)
        sc = lax.dot_general(qh, kh, (((1,), (1,)), ((), ())),
                             preferred_element_type=F32) * p
        a = jnp.exp(inter_h - m_t)
        c_old = c_ref[hd]
        n_old = n_ref[hd]
        num = a * _dot(qh, c_old.astype(BF16)) + _dot(sc.astype(BF16), vh)
        qn = jnp.sum(qh.astype(F32) * n_old, axis=-1, keepdims=True)
        den = a * qn + jnp.sum(sc, axis=-1, keepdims=True)
        hc = num / jnp.maximum(jnp.abs(den), jnp.exp(-m_t))
        wk = w_all[:, hd:hd + 1] * kh.astype(F32)
        dec = decay[:, hd:hd + 1]
        c_ref[hd] = dec * c_old + lax.dot_general(
            wk.astype(BF16), vh, (((0,), (0,)), ((), ())),
            preferred_element_type=F32)
        n_ref[hd] = dec * n_old + jnp.sum(wk, axis=0, keepdims=True)
        hn = hc * lax.rsqrt(jnp.mean(hc * hc, axis=-1, keepdims=True) + EPS)
        vs = slice(hd * ML_V_DIM, (hd + 1) * ML_V_DIM)
        y_ref[:, vs] = (hn * ng_ref[:, vs]
                        * jax.nn.sigmoid(o_ref[:, vs])).astype(BF16)

    out = _dot(y_ref[...], wout_ref[...])
    out_ref[...] = x_ref[...] + (1.0 + mod_ref[2:3, :]) * out


def _ml_cell(x, mods, layer, q, k, v, o, gates, gate_bias, norm_g, w_out, j):
    bsz, s, d = x.shape
    L = SEQ_TILE
    tok = lambda b, i: (b, i, 0)
    return pl.pallas_call(
        _ml_cell_kernel,
        grid=(bsz, s // L),
        in_specs=[
            pl.BlockSpec((None, L, d), tok),
            pl.BlockSpec((None, None, 6, d), lambda b, i: (layer, b, 0, 0)),
            pl.BlockSpec((None, L, ML_QK_W), tok),
            pl.BlockSpec((None, L, ML_QK_W), tok),
            pl.BlockSpec((None, L, ML_V_W), tok),
            pl.BlockSpec((None, L, ML_V_W), tok),
            pl.BlockSpec((None, L, V7X_LANES), tok),
            pl.BlockSpec((None, 1, V7X_LANES), lambda b, i: (j, 0, 0)),
            pl.BlockSpec((None, 1, ML_V_W), lambda b, i: (j, 0, 0)),
            _resident((None, ML_V_W, d), lambda b, i: (j, 0, 0)),
        ],
        out_specs=pl.BlockSpec((None, L, d), tok),
        out_shape=jax.ShapeDtypeStruct(x.shape, F32),
        scratch_shapes=[
            pltpu.VMEM((ML_HEADS, ML_QK_DIM, ML_V_DIM), F32),
            pltpu.VMEM((ML_HEADS, 1, ML_QK_DIM), F32),
            pltpu.VMEM((1, V7X_LANES), F32),
            pltpu.VMEM((L, ML_V_W), BF16),
        ],
        compiler_params=_compiler_params(),
        name=f"mlstm_cell{layer}",
    )(x, mods, q, k, v, o, gates, gate_bias, norm_g, w_out)


def kernel(x, c, ada_w, ada_b, norm_mix_g, norm_ffn_g, sc_w_in, sc_conv_w, sc_w_out, ml_w_in, ml_b_i, ml_b_f, ml_norm_g, ml_w_out, ffn_w_up, ffn_conv_w, ffn_conv_b, ffn_w_down, final_norm_g):
    mods = _ada_modulation(c, ada_w, ada_b)

    sc_w_in_b = sc_w_in.astype(BF16)
    sc_w_out_b = sc_w_out.astype(BF16)
    n_main = 2 * ML_QK_W + 2 * ML_V_W
    ml_w_main = ml_w_in[:, :, :n_main].astype(BF16)
    n_ml = ml_w_in.shape[0]
    ml_w_gate = jnp.pad(ml_w_in[:, :, n_main:],
                        ((0, 0), (0, 0), (0, V7X_LANES - 2 * ML_HEADS))).astype(BF16)
    ml_gate_bias = jnp.pad(jnp.concatenate([ml_b_i, ml_b_f], axis=-1),
                           ((0, 0), (0, V7X_LANES - 2 * ML_HEADS)))
    ml_gate_bias = ml_gate_bias.reshape(n_ml, 1, V7X_LANES)
    ml_norm_g = ml_norm_g.reshape(n_ml, 1, ML_V_W)
    norm_mix_g = norm_mix_g.reshape(DEPTH, 1, D_MODEL)
    norm_ffn_g = norm_ffn_g.reshape(DEPTH, 1, D_MODEL)
    ml_w_out_b = ml_w_out.astype(BF16)
    ffn_w_up_b = ffn_w_up.astype(BF16)
    ffn_w_down_b = ffn_w_down.astype(BF16)

    for layer in range(DEPTH):
        j = layer // 2
        if layer % 2 == 0:
            x = _sconv_layer(x, mods, layer, norm_mix_g, sc_w_in_b, sc_conv_w,
                             sc_w_out_b, j)
        else:
            q, k, v, o, gates = _ml_proj(x, mods, layer, norm_mix_g, ml_w_main,
                                         ml_w_gate, j)
            x = _ml_cell(x, mods, layer, q, k, v, o, gates, ml_gate_bias,
                         ml_norm_g, ml_w_out_b, j)
        x = _ffn_layer(x, mods, layer, norm_ffn_g, ffn_w_up_b, ffn_conv_w,
                       ffn_conv_b, ffn_w_down_b, final_norm_g,
                       final_norm=(layer == DEPTH - 1))
    return x
```

```python
import functools

import jax
import jax.numpy as jnp
from jax import lax
from jax.experimental import pallas as pl
from jax.experimental.pallas import tpu as pltpu

D_MODEL = 1024
DEPTH = 4
CONV_WIDTH = 3
ML_HEADS = 4
ML_V_DIM = D_MODEL // ML_HEADS
ML_QK_DIM = ML_V_DIM // 2
ML_QK_W = ML_HEADS * ML_QK_DIM
ML_V_W = ML_HEADS * ML_V_DIM
GATE_SOFTCAP = 15.0
D_FF = 2816
EPS = 1e-6

V7X_LANES = 128
V7X_SUBLANES = 8
V7X_MXU_DIM = 256
V7X_VMEM_BYTES = 64 * 1024 * 1024
VMEM_LIMIT_BYTES = V7X_VMEM_BYTES - 8 * 1024 * 1024

SEQ_TILE = 256
FFN_SEQ_TILE = 512
FF_CHUNK = V7X_MXU_DIM
SC_CHUNK = V7X_MXU_DIM
ADA_TILE = 1536

BF16 = jnp.bfloat16
F32 = jnp.float32


def _dot(a, b):
    return jnp.dot(a, b, preferred_element_type=F32)


def _norm_modulate(x, g, shift, scale):
    ms = jnp.mean(x * x, axis=-1, keepdims=True)
    y = x * lax.rsqrt(ms + EPS)
    return (y * g) * (1.0 + scale) + shift


def _causal_conv3(p, prev, w):
    t = p.shape[0]
    ext = jnp.concatenate([prev, p], axis=0)
    p1 = ext[V7X_SUBLANES - 1:V7X_SUBLANES - 1 + t]
    p2 = ext[V7X_SUBLANES - 2:V7X_SUBLANES - 2 + t]
    return w[0:1] * p2 + w[1:2] * p1 + w[2:3] * p


def _resident(block_shape, index_map):
    return pl.BlockSpec(block_shape, index_map, pipeline_mode=pl.Buffered(1))


def _compiler_params():
    return pltpu.CompilerParams(
        dimension_semantics=("arbitrary", "arbitrary"),
        vmem_limit_bytes=VMEM_LIMIT_BYTES)


def _ada_kernel(c_ref, w_ref, b_ref, o_ref):
    c = c_ref[...]
    cond = (c * jax.nn.sigmoid(c)).astype(BF16)
    o_ref[...] = _dot(cond, w_ref[...].astype(BF16)) + b_ref[...]


def _ada_modulation(c, ada_w, ada_b):
    bsz, d = c.shape
    n = ada_w.shape[-1]
    out = pl.pallas_call(
        _ada_kernel,
        grid=(DEPTH, n // ADA_TILE),
        in_specs=[
            pl.BlockSpec((bsz, d), lambda l, j: (0, 0)),
            pl.BlockSpec((None, d, ADA_TILE), lambda l, j: (l, 0, j)),
            pl.BlockSpec((None, 1, ADA_TILE), lambda l, j: (l, 0, j)),
        ],
        out_specs=pl.BlockSpec((None, bsz, ADA_TILE), lambda l, j: (l, 0, j)),
        out_shape=jax.ShapeDtypeStruct((DEPTH, bsz, n), F32),
        compiler_params=_compiler_params(),
        name="adaln_mod",
    )(c, ada_w, ada_b.reshape(DEPTH, 1, n))
    return out.reshape(DEPTH, bsz, 6, d)


def _sconv_kernel(x_ref, mod_ref, g_ref, win_ref, cw_ref, wout_ref, o_ref,
                  tail_ref, y_ref):
    @pl.when(pl.program_id(1) == 0)
    def _():
        tail_ref[...] = jnp.zeros_like(tail_ref)

    x = x_ref[...]
    t = x.shape[0]
    mod = mod_ref[...]
    h = _norm_modulate(x, g_ref[...], mod[0:1], mod[1:2]).astype(BF16)
    for j in range(D_MODEL // SC_CHUNK):
        lo = j * SC_CHUNK
        gb = _dot(h, win_ref[:, lo:lo + SC_CHUNK])
        gc = _dot(h, win_ref[:, D_MODEL + lo:D_MODEL + lo + SC_CHUNK])
        u = _dot(h, win_ref[:, 2 * D_MODEL + lo:2 * D_MODEL + lo + SC_CHUNK])
        v = gc * u
        prev = tail_ref[:, lo:lo + SC_CHUNK]
        tail_ref[:, lo:lo + SC_CHUNK] = v[t - V7X_SUBLANES:]
        conv = _causal_conv3(v, prev, cw_ref[:, lo:lo + SC_CHUNK])
        y_ref[:, lo:lo + SC_CHUNK] = (gb * conv).astype(BF16)
    out = _dot(y_ref[...], wout_ref[...])
    o_ref[...] = x + (1.0 + mod[2:3]) * out


def _sconv_layer(x, mods, layer, norm_g, w_in, conv_w, w_out, j):
    bsz, s, d = x.shape
    tm = SEQ_TILE
    return pl.pallas_call(
        _sconv_kernel,
        grid=(bsz, s // tm),
        in_specs=[
            pl.BlockSpec((None, tm, d), lambda b, i: (b, i, 0)),
            pl.BlockSpec((None, None, 6, d), lambda b, i: (layer, b, 0, 0)),
            pl.BlockSpec((None, 1, d), lambda b, i: (layer, 0, 0)),
            _resident((None, d, 3 * d), lambda b, i: (j, 0, 0)),
            pl.BlockSpec((None, CONV_WIDTH, d), lambda b, i: (j, 0, 0)),
            _resident((None, d, d), lambda b, i: (j, 0, 0)),
        ],
        out_specs=pl.BlockSpec((None, tm, d), lambda b, i: (b, i, 0)),
        out_shape=jax.ShapeDtypeStruct(x.shape, F32),
        scratch_shapes=[
            pltpu.VMEM((V7X_SUBLANES, d), F32),
            pltpu.VMEM((tm, d), BF16),
        ],
        compiler_params=_compiler_params(),
        name=f"sconv_layer{layer}",
    )(x, mods, norm_g, w_in, conv_w, w_out)


def _ffn_kernel(x_ref, mod_ref, g_ref, wup_ref, cw_ref, cb_ref, wdown_ref,
                fg_ref, o_ref, tail_ref, a_ref, *, final_norm):
    @pl.when(pl.program_id(1) == 0)
    def _():
        tail_ref[...] = jnp.zeros_like(tail_ref)

    x = x_ref[...]
    t = x.shape[0]
    mod = mod_ref[...]
    h = _norm_modulate(x, g_ref[...], mod[3:4], mod[4:5]).astype(BF16)
    for j in range(D_FF // FF_CHUNK):
        halves = []
        for base in (0, D_FF):
            lo = base + j * FF_CHUNK
            p = _dot(h, wup_ref[:, lo:lo + FF_CHUNK])
            prev = tail_ref[:, lo:lo + FF_CHUNK]
            tail_ref[:, lo:lo + FF_CHUNK] = p[t - V7X_SUBLANES:]
            halves.append(_causal_conv3(p, prev, cw_ref[:, lo:lo + FF_CHUNK])
                          + cb_ref[:, lo:lo + FF_CHUNK])
        gate, val = halves
        act = gate * jax.nn.sigmoid(gate) * val
        a_ref[:, j * FF_CHUNK:(j + 1) * FF_CHUNK] = act.astype(BF16)
    out = _dot(a_ref[...], wdown_ref[...])
    y = x + (1.0 + mod[5:6]) * out
    if final_norm:
        ms = jnp.mean(y * y, axis=-1, keepdims=True)
        y = y * lax.rsqrt(ms + EPS) * fg_ref[...]
    o_ref[...] = y


def _ffn_layer(x, mods, layer, norm_g, w_up, conv_w, conv_b, w_down, final_g,
               final_norm):
    bsz, s, d = x.shape
    tm = FFN_SEQ_TILE
    f2 = 2 * D_FF
    return pl.pallas_call(
        functools.partial(_ffn_kernel, final_norm=final_norm),
        grid=(bsz, s // tm),
        in_specs=[
            pl.BlockSpec((None, tm, d), lambda b, i: (b, i, 0)),
            pl.BlockSpec((None, None, 6, d), lambda b, i: (layer, b, 0, 0)),
            pl.BlockSpec((None, 1, d), lambda b, i: (layer, 0, 0)),
            _resident((None, d, f2), lambda b, i: (layer, 0, 0)),
            pl.BlockSpec((None, CONV_WIDTH, f2), lambda b, i: (layer, 0, 0)),
            pl.BlockSpec((None, 1, f2), lambda b, i: (layer, 0, 0)),
            _resident((None, D_FF, d), lambda b, i: (layer, 0, 0)),
            pl.BlockSpec((1, d), lambda b, i: (0, 0)),
        ],
        out_specs=pl.BlockSpec((None, tm, d), lambda b, i: (b, i, 0)),
        out_shape=jax.ShapeDtypeStruct(x.shape, F32),
        scratch_shapes=[
            pltpu.VMEM((V7X_SUBLANES, f2), F32),
            pltpu.VMEM((tm, D_FF), BF16),
        ],
        compiler_params=_compiler_params(),
        name=f"ffn_layer{layer}",
    )(x, mods, norm_g, w_up, conv_w, conv_b.reshape(DEPTH, 1, f2), w_down,
      final_g.reshape(1, d))


def _ml_proj_kernel(x_ref, mod_ref, g_ref, win_ref, wg_ref,
                    q_ref, k_ref, v_ref, o_ref, gate_ref):
    x = x_ref[...]
    mod = mod_ref[...]
    h = _norm_modulate(x, g_ref[...], mod[0:1], mod[1:2]).astype(BF16)
    q_ref[...] = (_dot(h, win_ref[:, 0:ML_QK_W])
                  * (ML_QK_DIM ** -0.5)).astype(BF16)
    k_ref[...] = _dot(h, win_ref[:, ML_QK_W:2 * ML_QK_W]).astype(BF16)
    v0 = 2 * ML_QK_W
    v_ref[...] = _dot(h, win_ref[:, v0:v0 + ML_V_W]).astype(BF16)
    o_ref[...] = _dot(h, win_ref[:, v0 + ML_V_W:v0 + 2 * ML_V_W])
    gate_ref[...] = _dot(h, wg_ref[...])


def _ml_proj(x, mods, layer, norm_g, w_main, w_gate, j):
    bsz, s, d = x.shape
    tm = SEQ_TILE
    nmain = w_main.shape[-1]
    tok = lambda b, i: (b, i, 0)
    return pl.pallas_call(
        _ml_proj_kernel,
        grid=(bsz, s // tm),
        in_specs=[
            pl.BlockSpec((None, tm, d), tok),
            pl.BlockSpec((None, None, 6, d), lambda b, i: (layer, b, 0, 0)),
            pl.BlockSpec((None, 1, d), lambda b, i: (layer, 0, 0)),
            _resident((None, d, nmain), lambda b, i: (j, 0, 0)),
            _resident((None, d, V7X_LANES), lambda b, i: (j, 0, 0)),
        ],
        out_specs=[
            pl.BlockSpec((None, tm, ML_QK_W), tok),
            pl.BlockSpec((None, tm, ML_QK_W), tok),
            pl.BlockSpec((None, tm, ML_V_W), tok),
            pl.BlockSpec((None, tm, ML_V_W), tok),
            pl.BlockSpec((None, tm, V7X_LANES), tok),
        ],
        out_shape=[
            jax.ShapeDtypeStruct((bsz, s, ML_QK_W), BF16),
            jax.ShapeDtypeStruct((bsz, s, ML_QK_W), BF16),
            jax.ShapeDtypeStruct((bsz, s, ML_V_W), BF16),
            jax.ShapeDtypeStruct((bsz, s, ML_V_W), F32),
            jax.ShapeDtypeStruct((bsz, s, V7X_LANES), F32),
        ],
        compiler_params=_compiler_params(),
        name=f"mlstm_proj{layer}",
    )(x, mods, norm_g, w_main, w_gate)


def _log_sigmoid(z):
    return jnp.minimum(z, 0.0) - jnp.log1p(jnp.exp(-jnp.abs(z)))


def _ml_cell_kernel(x_ref, mod_ref, q_ref, k_ref, v_ref, o_ref, gate_ref,
                    gbias_ref, ng_ref, wout_ref, out_ref,
                    c_ref, n_ref, m_ref, y_ref):
    @pl.when(pl.program_id(1) == 0)
    def _():
        c_ref[...] = jnp.zeros_like(c_ref)
        n_ref[...] = jnp.zeros_like(n_ref)
        m_ref[...] = jnp.zeros_like(m_ref)

    L = q_ref.shape[0]
    row = lax.broadcasted_iota(jnp.int32, (L, L), 0)
    col = lax.broadcasted_iota(jnp.int32, (L, L), 1)
    causal = row >= col
    tril = jnp.where(causal, 1.0, 0.0).astype(BF16)

    capped = GATE_SOFTCAP * jnp.tanh((gate_ref[...] + gbias_ref[...])
                                     * (1.0 / GATE_SOFTCAP))
    i_log = capped
    f_log = _log_sigmoid(pltpu.roll(capped, V7X_LANES - ML_HEADS, axis=1))
    f_hi = f_log.astype(BF16)
    r1 = f_log - f_hi.astype(F32)
    f_mid = r1.astype(BF16)
    f_lo = (r1 - f_mid.astype(F32)).astype(BF16)
    b = _dot(tril, f_hi) + _dot(tril, f_mid) + _dot(tril, f_lo)
    r = i_log - b
    r_t = r.T
    m_prev = m_ref[...]
    b_last = b[L - 1:L, :]
    inter = b + m_prev
    m_new = b_last + jnp.maximum(m_prev, jnp.max(r, axis=0, keepdims=True))
    decay = jnp.exp(b_last + m_prev - m_new)
    w_all = jnp.exp(b_last + r - m_new)
    m_ref[...] = m_new

    for hd in range(ML_HEADS):
        qh = q_ref[:, hd * ML_QK_DIM:(hd + 1) * ML_QK_DIM]
        kh = k_ref[:, hd * ML_QK_DIM:(hd + 1) * ML_QK_DIM]
        vh = v_ref[:, hd * ML_V_DIM:(hd + 1) * ML_V_DIM]
        d_log = jnp.where(causal, b[:, hd:hd + 1] + r_t[hd:hd + 1, :], -jnp.inf)
        inter_h = inter[:, hd:hd + 1]
        m_t = jnp.maximum(inter_h, jnp.max(d_log, axis=-1, keepdims=True))
        p = jnp.exp(d_log - m_t)
        sc = lax.dot_general(qh, kh, (((1,), (1,)), ((), ())),
                             preferred_element_type=F32) * p
        a = jnp.exp(inter_h - m_t)
        c_old = c_ref[hd]
        n_old = n_ref[hd]
        num = a * _dot(qh, c_old.astype(BF16)) + _dot(sc.astype(BF16), vh)
        qn = jnp.sum(qh.astype(F32) * n_old, axis=-1, keepdims=True)
        den = a * qn + jnp.sum(sc, axis=-1, keepdims=True)
        hc = num / jnp.maximum(jnp.abs(den), jnp.exp(-m_t))
        wk = w_all[:, hd:hd + 1] * kh.astype(F32)
        dec = decay[:, hd:hd + 1]
        c_ref[hd] = dec * c_old + lax.dot_general(
            wk.astype(BF16), vh, (((0,), (0,)), ((), ())),
            preferred_element_type=F32)
        n_ref[hd] = dec * n_old + jnp.sum(wk, axis=0, keepdims=True)
        hn = hc * lax.rsqrt(jnp.mean(hc * hc, axis=-1, keepdims=True) + EPS)
        vs = slice(hd * ML_V_DIM, (hd + 1) * ML_V_DIM)
        y_ref[:, vs] = (hn * ng_ref[:, vs]
                        * jax.nn.sigmoid(o_ref[:, vs])).astype(BF16)

    out = _dot(y_ref[...], wout_ref[...])
    out_ref[...] = x_ref[...] + (1.0 + mod_ref[2:3, :]) * out


def _ml_cell(x, mods, layer, q, k, v, o, gates, gate_bias, norm_g, w_out, j):
    bsz, s, d = x.shape
    L = SEQ_TILE
    tok = lambda b, i: (b, i, 0)
    return pl.pallas_call(
        _ml_cell_kernel,
        grid=(bsz, s // L),
        in_specs=[
            pl.BlockSpec((None, L, d), tok),
            pl.BlockSpec((None, None, 6, d), lambda b, i: (layer, b, 0, 0)),
            pl.BlockSpec((None, L, ML_QK_W), tok),
            pl.BlockSpec((None, L, ML_QK_W), tok),
            pl.BlockSpec((None, L, ML_V_W), tok),
            pl.BlockSpec((None, L, ML_V_W), tok),
            pl.BlockSpec((None, L, V7X_LANES), tok),
            pl.BlockSpec((None, 1, V7X_LANES), lambda b, i: (j, 0, 0)),
            pl.BlockSpec((None, 1, ML_V_W), lambda b, i: (j, 0, 0)),
            _resident((None, ML_V_W, d), lambda b, i: (j, 0, 0)),
        ],
        out_specs=pl.BlockSpec((None, L, d), tok),
        out_shape=jax.ShapeDtypeStruct(x.shape, F32),
        scratch_shapes=[
            pltpu.VMEM((ML_HEADS, ML_QK_DIM, ML_V_DIM), F32),
            pltpu.VMEM((ML_HEADS, 1, ML_QK_DIM), F32),
            pltpu.VMEM((1, V7X_LANES), F32),
            pltpu.VMEM((L, ML_V_W), BF16),
        ],
        compiler_params=_compiler_params(),
        name=f"mlstm_cell{layer}",
    )(x, mods, q, k, v, o, gates, gate_bias, norm_g, w_out)


def kernel(x, c, ada_w, ada_b, norm_mix_g, norm_ffn_g, sc_w_in, sc_conv_w, sc_w_out, ml_w_in, ml_b_i, ml_b_f, ml_norm_g, ml_w_out, ffn_w_up, ffn_conv_w, ffn_conv_b, ffn_w_down, final_norm_g):
    mods = _ada_modulation(c, ada_w, ada_b)

    sc_w_in_b = sc_w_in.astype(BF16)
    sc_w_out_b = sc_w_out.astype(BF16)
    n_main = 2 * ML_QK_W + 2 * ML_V_W
    ml_w_main = ml_w_in[:, :, :n_main].astype(BF16)
    n_ml = ml_w_in.shape[0]
    ml_w_gate = jnp.pad(ml_w_in[:, :, n_main:],
                        ((0, 0), (0, 0), (0, V7X_LANES - 2 * ML_HEADS))).astype(BF16)
    ml_gate_bias = jnp.pad(jnp.concatenate([ml_b_i, ml_b_f], axis=-1),
                           ((0, 0), (0, V7X_LANES - 2 * ML_HEADS)))
    ml_gate_bias = ml_gate_bias.reshape(n_ml, 1, V7X_LANES)
    ml_norm_g = ml_norm_g.reshape(n_ml, 1, ML_V_W)
    norm_mix_g = norm_mix_g.reshape(DEPTH, 1, D_MODEL)
    norm_ffn_g = norm_ffn_g.reshape(DEPTH, 1, D_MODEL)
    ml_w_out_b = ml_w_out.astype(BF16)
    ffn_w_up_b = ffn_w_up.astype(BF16)
    ffn_w_down_b = ffn_w_down.astype(BF16)

    for layer in range(DEPTH):
        j = layer // 2
        if layer % 2 == 0:
            x = _sconv_layer(x, mods, layer, norm_mix_g, sc_w_in_b, sc_conv_w,
                             sc_w_out_b, j)
        else:
            q, k, v, o, gates = _ml_proj(x, mods, layer, norm_mix_g, ml_w_main,
                                         ml_w_gate, j)
            x = _ml_cell(x, mods, layer, q, k, v, o, gates, ml_gate_bias,
                         ml_norm_g, ml_w_out_b, j)
        x = _ffn_layer(x, mods, layer, norm_ffn_g, ffn_w_up_b, ffn_conv_w,
                       ffn_conv_b, ffn_w_down_b, final_norm_g,
                       final_norm=(layer == DEPTH - 1))
    return x
```

```python
import functools

import jax
import jax.numpy as jnp
from jax import lax
from jax.experimental import pallas as pl
from jax.experimental.pallas import tpu as pltpu

D_MODEL = 1024
DEPTH = 4
CONV_WIDTH = 3
ML_HEADS = 4
ML_V_DIM = D_MODEL // ML_HEADS
ML_QK_DIM = ML_V_DIM // 2
ML_QK_W = ML_HEADS * ML_QK_DIM
ML_V_W = ML_HEADS * ML_V_DIM
GATE_SOFTCAP = 15.0
D_FF = 2816
EPS = 1e-6

V7X_LANES = 128
V7X_SUBLANES = 8
V7X_MXU_DIM = 256
V7X_VMEM_BYTES = 64 * 1024 * 1024
VMEM_LIMIT_BYTES = V7X_VMEM_BYTES - 8 * 1024 * 1024

SEQ_TILE = 256
FFN_SEQ_TILE = 512
FF_CHUNK = V7X_MXU_DIM
SC_CHUNK = V7X_MXU_DIM
ADA_TILE = 1536

BF16 = jnp.bfloat16
F32 = jnp.float32


def _dot(a, b):
    return jnp.dot(a, b, preferred_element_type=F32)


def _norm_modulate(x, g, shift, scale):
    ms = jnp.mean(x * x, axis=-1, keepdims=True)
    y = x * lax.rsqrt(ms + EPS)
    return (y * g) * (1.0 + scale) + shift


def _causal_conv3(p, prev, w):
    t = p.shape[0]
    ext = jnp.concatenate([prev, p], axis=0)
    p1 = ext[V7X_SUBLANES - 1:V7X_SUBLANES - 1 + t]
    p2 = ext[V7X_SUBLANES - 2:V7X_SUBLANES - 2 + t]
    return w[0:1] * p2 + w[1:2] * p1 + w[2:3] * p


def _resident(block_shape, index_map):
    return pl.BlockSpec(block_shape, index_map, pipeline_mode=pl.Buffered(1))


def _compiler_params():
    return pltpu.CompilerParams(
        dimension_semantics=("arbitrary", "arbitrary"),
        vmem_limit_bytes=VMEM_LIMIT_BYTES)


def _ada_kernel(c_ref, w_ref, b_ref, o_ref):
    c = c_ref[...]
    cond = (c * jax.nn.sigmoid(c)).astype(BF16)
    o_ref[...] = _dot(cond, w_ref[...].astype(BF16)) + b_ref[...]


def _ada_modulation(c, ada_w, ada_b):
    bsz, d = c.shape
    n = ada_w.shape[-1]
    out = pl.pallas_call(
        _ada_kernel,
        grid=(DEPTH, n // ADA_TILE),
        in_specs=[
            pl.BlockSpec((bsz, d), lambda l, j: (0, 0)),
            pl.BlockSpec((None, d, ADA_TILE), lambda l, j: (l, 0, j)),
            pl.BlockSpec((None, 1, ADA_TILE), lambda l, j: (l, 0, j)),
        ],
        out_specs=pl.BlockSpec((None, bsz, ADA_TILE), lambda l, j: (l, 0, j)),
        out_shape=jax.ShapeDtypeStruct((DEPTH, bsz, n), F32),
        compiler_params=_compiler_params(),
        name="adaln_mod",
    )(c, ada_w, ada_b.reshape(DEPTH, 1, n))
    return out.reshape(DEPTH, bsz, 6, d)


def _sconv_kernel(x_ref, mod_ref, g_ref, win_ref, cw_ref, wout_ref, o_ref,
                  tail_ref, y_ref):
    @pl.when(pl.program_id(1) == 0)
    def _():
        tail_ref[...] = jnp.zeros_like(tail_ref)

    x = x_ref[...]
    t = x.shape[0]
    mod = mod_ref[...]
    h = _norm_modulate(x, g_ref[...], mod[0:1], mod[1:2]).astype(BF16)
    for j in range(D_MODEL // SC_CHUNK):
        lo = j * SC_CHUNK
        gb = _dot(h, win_ref[:, lo:lo + SC_CHUNK])
        gc = _dot(h, win_ref[:, D_MODEL + lo:D_MODEL + lo + SC_CHUNK])
        u = _dot(h, win_ref[:, 2 * D_MODEL + lo:2 * D_MODEL + lo + SC_CHUNK])
        v = gc * u
        prev = tail_ref[:, lo:lo + SC_CHUNK]
        tail_ref[:, lo:lo + SC_CHUNK] = v[t - V7X_SUBLANES:]
        conv = _causal_conv3(v, prev, cw_ref[:, lo:lo + SC_CHUNK])
        y_ref[:, lo:lo + SC_CHUNK] = (gb * conv).astype(BF16)
    out = _dot(y_ref[...], wout_ref[...])
    o_ref[...] = x + (1.0 + mod[2:3]) * out


def _sconv_layer(x, mods, layer, norm_g, w_in, conv_w, w_out, j):
    bsz, s, d = x.shape
    tm = SEQ_TILE
    return pl.pallas_call(
        _sconv_kernel,
        grid=(bsz, s // tm),
        in_specs=[
            pl.BlockSpec((None, tm, d), lambda b, i: (b, i, 0)),
            pl.BlockSpec((None, None, 6, d), lambda b, i: (layer, b, 0, 0)),
            pl.BlockSpec((None, 1, d), lambda b, i: (layer, 0, 0)),
            _resident((None, d, 3 * d), lambda b, i: (j, 0, 0)),
            pl.BlockSpec((None, CONV_WIDTH, d), lambda b, i: (j, 0, 0)),
            _resident((None, d, d), lambda b, i: (j, 0, 0)),
        ],
        out_specs=pl.BlockSpec((None, tm, d), lambda b, i: (b, i, 0)),
        out_shape=jax.ShapeDtypeStruct(x.shape, F32),
        scratch_shapes=[
            pltpu.VMEM((V7X_SUBLANES, d), F32),
            pltpu.VMEM((tm, d), BF16),
        ],
        compiler_params=_compiler_params(),
        name=f"sconv_layer{layer}",
    )(x, mods, norm_g, w_in, conv_w, w_out)


def _ffn_kernel(x_ref, mod_ref, g_ref, wup_ref, cw_ref, cb_ref, wdown_ref,
                fg_ref, o_ref, tail_ref, a_ref, *, final_norm):
    @pl.when(pl.program_id(1) == 0)
    def _():
        tail_ref[...] = jnp.zeros_like(tail_ref)

    x = x_ref[...]
    t = x.shape[0]
    mod = mod_ref[...]
    h = _norm_modulate(x, g_ref[...], mod[3:4], mod[4:5]).astype(BF16)
    for j in range(D_FF // FF_CHUNK):
        halves = []
        for base in (0, D_FF):
            lo = base + j * FF_CHUNK
            p = _dot(h, wup_ref[:, lo:lo + FF_CHUNK])
            prev = tail_ref[:, lo:lo + FF_CHUNK]
            tail_ref[:, lo:lo + FF_CHUNK] = p[t - V7X_SUBLANES:]
            halves.append(_causal_conv3(p, prev, cw_ref[:, lo:lo + FF_CHUNK])
                          + cb_ref[:, lo:lo + FF_CHUNK])
        gate, val = halves
        act = gate * jax.nn.sigmoid(gate) * val
        a_ref[:, j * FF_CHUNK:(j + 1) * FF_CHUNK] = act.astype(BF16)
    out = _dot(a_ref[...], wdown_ref[...])
    y = x + (1.0 + mod[5:6]) * out
    if final_norm:
        ms = jnp.mean(y * y, axis=-1, keepdims=True)
        y = y * lax.rsqrt(ms + EPS) * fg_ref[...]
    o_ref[...] = y


def _ffn_layer(x, mods, layer, norm_g, w_up, conv_w, conv_b, w_down, final_g,
               final_norm):
    bsz, s, d = x.shape
    tm = FFN_SEQ_TILE
    f2 = 2 * D_FF
    return pl.pallas_call(
        functools.partial(_ffn_kernel, final_norm=final_norm),
        grid=(bsz, s // tm),
        in_specs=[
            pl.BlockSpec((None, tm, d), lambda b, i: (b, i, 0)),
            pl.BlockSpec((None, None, 6, d), lambda b, i: (layer, b, 0, 0)),
            pl.BlockSpec((None, 1, d), lambda b, i: (layer, 0, 0)),
            _resident((None, d, f2), lambda b, i: (layer, 0, 0)),
            pl.BlockSpec((None, CONV_WIDTH, f2), lambda b, i: (layer, 0, 0)),
            pl.BlockSpec((None, 1, f2), lambda b, i: (layer, 0, 0)),
            _resident((None, D_FF, d), lambda b, i: (layer, 0, 0)),
            pl.BlockSpec((1, d), lambda b, i: (0, 0)),
        ],
        out_specs=pl.BlockSpec((None, tm, d), lambda b, i: (b, i, 0)),
        out_shape=jax.ShapeDtypeStruct(x.shape, F32),
        scratch_shapes=[
            pltpu.VMEM((V7X_SUBLANES, f2), F32),
            pltpu.VMEM((tm, D_FF), BF16),
        ],
        compiler_params=_compiler_params(),
        name=f"ffn_layer{layer}",
    )(x, mods, norm_g, w_up, conv_w, conv_b.reshape(DEPTH, 1, f2), w_down,
      final_g.reshape(1, d))


ML_STATE_ROWS = ML_V_DIM + 16
ML_FM_ROWS = ML_QK_W + 2 * ML_V_W
ML_GATE_ROWS = 2 * V7X_SUBLANES


def _log_sigmoid(z):
    return jnp.minimum(z, 0.0) - jnp.log1p(jnp.exp(-jnp.abs(z)))


def _softcap(z):
    return GATE_SOFTCAP * jnp.tanh(z * (1.0 / GATE_SOFTCAP))


def _split3(z):
    hi = z.astype(BF16).astype(F32)
    r1 = z - hi
    mid = r1.astype(BF16).astype(F32)
    lo = (r1 - mid).astype(BF16).astype(F32)
    return hi, mid, lo


def _lane_tile(a, reps):
    return jnp.concatenate([a] * reps, axis=1)


def _dot_nt(a, b):
    return lax.dot_general(a, b, (((1,), (1,)), ((), ())),
                           preferred_element_type=F32)


def _dot_tn(a, b):
    return lax.dot_general(a, b, (((0,), (0,)), ((), ())),
                           preferred_element_type=F32)


def _mlstm_kernel(x_ref, mod_ref, g_ref, wfm_ref, wk_ref, wg_ref, wgt_ref,
                  gb_ref, gbt_ref, ng_ref, wout_ref, out_ref,
                  state_ref, m_ref, y_ref):
    @pl.when(pl.program_id(1) == 0)
    def _():
        state_ref[...] = jnp.zeros_like(state_ref)
        m_ref[...] = jnp.zeros_like(m_ref)

    x = x_ref[...]
    L = x.shape[0]
    reps = L // V7X_LANES
    mod = mod_ref[...]
    h = _norm_modulate(x, g_ref[...], mod[0:1], mod[1:2]).astype(BF16)

    row = lax.broadcasted_iota(jnp.int32, (L, L), 0)
    col = lax.broadcasted_iota(jnp.int32, (L, L), 1)
    lower = jnp.where(row >= col, 1.0, 0.0).astype(BF16)
    upper = jnp.where(row <= col, 1.0, 0.0).astype(BF16)
    key_before_query = row <= col

    capped = _softcap(_dot(h, wg_ref[...]) + gb_ref[...])
    f_tm = _log_sigmoid(pltpu.roll(capped, V7X_LANES - ML_HEADS, axis=1))
    hi, mid, lo = _split3(f_tm)
    r_tm = capped - (_dot(lower, hi.astype(BF16)) + _dot(lower, mid.astype(BF16))
                     + _dot(lower, lo.astype(BF16)))

    g_fm = _dot_nt(wgt_ref[...], h) + _lane_tile(gbt_ref[...], reps)
    i_fm = _softcap(g_fm[0:V7X_SUBLANES])
    f_fm = _log_sigmoid(_softcap(g_fm[V7X_SUBLANES:]))
    hi, mid, lo = _split3(f_fm)
    stacked = jnp.concatenate([hi, mid, lo, jnp.zeros_like(hi)], axis=0)
    cs = _dot(stacked.astype(BF16), upper)
    b_fm = cs[0:8] + cs[8:16] + cs[16:24]
    r_fm = i_fm - b_fm
    m_prev = _lane_tile(m_ref[...], reps)
    b_last = jnp.broadcast_to(b_fm[:, L - 1:L], (V7X_SUBLANES, L))
    inter = b_fm + m_prev
    m_new = b_last + jnp.maximum(m_prev, jnp.max(r_fm, axis=1, keepdims=True))
    decay = jnp.exp(b_last + m_prev - m_new)
    w_fm = jnp.exp(b_last + r_fm - m_new)
    m_ref[...] = m_new[:, 0:V7X_LANES]

    k_all = _dot(h, wk_ref[...]).astype(BF16)
    ng = ng_ref[...]

    for hd in range(ML_HEADS):
        q0 = hd * ML_QK_DIM
        v0 = ML_QK_W + hd * ML_V_DIM
        o0 = ML_QK_W + ML_V_W + hd * ML_V_DIM
        qt = (_dot_nt(wfm_ref[q0:q0 + ML_QK_DIM, :], h)
              * (ML_QK_DIM ** -0.5)).astype(BF16)
        vt = _dot_nt(wfm_ref[v0:v0 + ML_V_DIM, :], h).astype(BF16)
        ot = _dot_nt(wfm_ref[o0:o0 + ML_V_DIM, :], h)
        kh = k_all[:, q0:q0 + ML_QK_DIM]

        d_log = jnp.where(key_before_query,
                          r_tm[:, hd:hd + 1] + b_fm[hd:hd + 1, :], -jnp.inf)
        inter_h = inter[hd:hd + 1, :]
        m_t = jnp.maximum(inter_h, jnp.max(d_log, axis=0, keepdims=True))
        p = jnp.exp(d_log - m_t)
        sc = _dot(kh, qt) * p
        a = jnp.exp(inter_h - m_t)
        st_old = state_ref[hd]
        cq = _dot(st_old.astype(BF16), qt)
        num = a * cq[0:ML_V_DIM] + _dot(vt, sc.astype(BF16))
        den = a * cq[ML_V_DIM:ML_V_DIM + 1] + jnp.sum(sc, axis=0, keepdims=True)
        hc = num * (1.0 / jnp.maximum(jnp.abs(den), jnp.exp(-m_t)))

        w_row = w_fm[hd:hd + 1, :]
        n_rows = jnp.where(
            lax.broadcasted_iota(jnp.int32, (2 * V7X_SUBLANES, L), 0) == 0,
            w_row, 0.0)
        vw = jnp.concatenate([vt.astype(F32) * w_row, n_rows],
                             axis=0).astype(BF16)
        state_ref[hd] = decay[hd:hd + 1, 0:ML_QK_DIM] * st_old + _dot(vw, kh)

        hn = hc * lax.rsqrt(jnp.mean(hc * hc, axis=0, keepdims=True) + EPS)
        ng_h = _lane_tile(ng[hd * ML_V_DIM:(hd + 1) * ML_V_DIM, :], reps)
        y_ref[hd * ML_V_DIM:(hd + 1) * ML_V_DIM, :] = (
            hn * ng_h * jax.nn.sigmoid(ot)).astype(BF16)

    out = _dot_tn(y_ref[...], wout_ref[...])
    out_ref[...] = x + (1.0 + mod[2:3]) * out


def _mlstm_layer(x, mods, layer, norm_g, w_fm, w_k, w_g, w_gt, gate_b, gate_bt,
                 ng_fm, w_out, j):
    bsz, s, d = x.shape
    L = SEQ_TILE
    tok = lambda b, i: (b, i, 0)
    per_j = lambda b, i: (j, 0, 0)
    return pl.pallas_call(
        _mlstm_kernel,
        grid=(bsz, s // L),
        in_specs=[
            pl.BlockSpec((None, L, d), tok),
            pl.BlockSpec((None, None, 6, d), lambda b, i: (layer, b, 0, 0)),
            pl.BlockSpec((None, 1, d), lambda b, i: (layer, 0, 0)),
            _resident((None, ML_FM_ROWS, d), per_j),
            _resident((None, d, ML_QK_W), per_j),
            _resident((None, d, V7X_LANES), per_j),
            _resident((None, ML_GATE_ROWS, d), per_j),
            pl.BlockSpec((None, 1, V7X_LANES), per_j),
            pl.BlockSpec((None, ML_GATE_ROWS, V7X_LANES), per_j),
            _resident((None, ML_V_W, V7X_LANES), per_j),
            _resident((None, ML_V_W, d), per_j),
        ],
        out_specs=pl.BlockSpec((None, L, d), tok),
        out_shape=jax.ShapeDtypeStruct(x.shape, F32),
        scratch_shapes=[
            pltpu.VMEM((ML_HEADS, ML_STATE_ROWS, ML_QK_DIM), F32),
            pltpu.VMEM((V7X_SUBLANES, V7X_LANES), F32),
            pltpu.VMEM((ML_V_W, L), BF16),
        ],
        compiler_params=_compiler_params(),
        name=f"mlstm_layer{layer}",
    )(x, mods, norm_g, w_fm, w_k, w_g, w_gt, gate_b, gate_bt, ng_fm, w_out)


def _mlstm_weights(ml_w_in, ml_b_i, ml_b_f, ml_norm_g):
    n_ml = ml_w_in.shape[0]
    q, k, v, o, gi, gf = jnp.split(
        ml_w_in, [ML_QK_W, 2 * ML_QK_W, 2 * ML_QK_W + ML_V_W,
                  2 * ML_QK_W + 2 * ML_V_W, 2 * ML_QK_W + 2 * ML_V_W + ML_HEADS],
        axis=-1)
    w_fm = jnp.concatenate([q, v, o], axis=-1).transpose(0, 2, 1).astype(BF16)
    w_k = k.astype(BF16)
    lane_pad = V7X_LANES - 2 * ML_HEADS
    w_g = jnp.pad(jnp.concatenate([gi, gf], axis=-1),
                  ((0, 0), (0, 0), (0, lane_pad))).astype(BF16)
    row_pad = V7X_SUBLANES - ML_HEADS
    pad_rows = lambda t: jnp.pad(t, ((0, 0), (0, row_pad), (0, 0)))
    w_gt = jnp.concatenate([pad_rows(gi.transpose(0, 2, 1)),
                            pad_rows(gf.transpose(0, 2, 1))], axis=1).astype(BF16)
    gate_b = jnp.pad(jnp.concatenate([ml_b_i, ml_b_f], axis=-1),
                     ((0, 0), (0, lane_pad))).reshape(n_ml, 1, V7X_LANES)
    bt = jnp.concatenate([pad_rows(ml_b_i[:, :, None]),
                          pad_rows(ml_b_f[:, :, None])], axis=1)
    gate_bt = jnp.broadcast_to(bt, (n_ml, ML_GATE_ROWS, V7X_LANES))
    ng_fm = jnp.broadcast_to(ml_norm_g[:, :, None], (n_ml, ML_V_W, V7X_LANES))
    return w_fm, w_k, w_g, w_gt, gate_b, gate_bt, ng_fm


def kernel(x, c, ada_w, ada_b, norm_mix_g, norm_ffn_g, sc_w_in, sc_conv_w, sc_w_out, ml_w_in, ml_b_i, ml_b_f, ml_norm_g, ml_w_out, ffn_w_up, ffn_conv_w, ffn_conv_b, ffn_w_down, final_norm_g):
    mods = _ada_modulation(c, ada_w, ada_b)

    sc_w_in_b = sc_w_in.astype(BF16)
    sc_w_out_b = sc_w_out.astype(BF16)
    ml_weights = _mlstm_weights(ml_w_in, ml_b_i, ml_b_f, ml_norm_g)
    ml_w_out_b = ml_w_out.astype(BF16)
    ffn_w_up_b = ffn_w_up.astype(BF16)
    ffn_w_down_b = ffn_w_down.astype(BF16)
    norm_mix_g = norm_mix_g.reshape(DEPTH, 1, D_MODEL)
    norm_ffn_g = norm_ffn_g.reshape(DEPTH, 1, D_MODEL)

    for layer in range(DEPTH):
        j = layer // 2
        if layer % 2 == 0:
            x = _sconv_layer(x, mods, layer, norm_mix_g, sc_w_in_b, sc_conv_w,
                             sc_w_out_b, j)
        else:
            x = _mlstm_layer(x, mods, layer, norm_mix_g, *ml_weights,
                             ml_w_out_b, j)
        x = _ffn_layer(x, mods, layer, norm_ffn_g, ffn_w_up_b, ffn_conv_w,
                       ffn_conv_b, ffn_w_down_b, final_norm_g,
                       final_norm=(layer == DEPTH - 1))
    return x
```

```python
import functools

import jax
import jax.numpy as jnp
from jax import lax
from jax.experimental import pallas as pl
from jax.experimental.pallas import tpu as pltpu

D_MODEL = 1024
DEPTH = 4
CONV_WIDTH = 3
ML_HEADS = 4
ML_V_DIM = D_MODEL // ML_HEADS
ML_QK_DIM = ML_V_DIM // 2
ML_QK_W = ML_HEADS * ML_QK_DIM
ML_V_W = ML_HEADS * ML_V_DIM
GATE_SOFTCAP = 15.0
D_FF = 2816
EPS = 1e-6

V7X_LANES = 128
V7X_SUBLANES = 8
V7X_MXU_DIM = 256
V7X_VMEM_BYTES = 64 * 1024 * 1024
VMEM_LIMIT_BYTES = V7X_VMEM_BYTES - 8 * 1024 * 1024

SEQ_TILE = 256
FFN_SEQ_TILE = 512
FF_CHUNK = V7X_MXU_DIM
SC_CHUNK = V7X_MXU_DIM
ADA_TILE = 1536

BF16 = jnp.bfloat16
F32 = jnp.float32


def _dot(a, b):
    return jnp.dot(a, b, preferred_element_type=F32)


def _norm_modulate(x, g, shift, scale):
    ms = jnp.mean(x * x, axis=-1, keepdims=True)
    y = x * lax.rsqrt(ms + EPS)
    return (y * g) * (1.0 + scale) + shift


def _causal_conv3(p, hist_ref, cols, w):
    t = p.shape[0]
    halo = V7X_SUBLANES
    hist_ref[0:halo, cols] = hist_ref[t:t + halo, cols]
    hist_ref[halo:halo + t, cols] = p
    p1 = hist_ref[halo - 1:halo - 1 + t, cols]
    p2 = hist_ref[halo - 2:halo - 2 + t, cols]
    return w[0:1] * p2 + w[1:2] * p1 + w[2:3] * p


def _resident(block_shape, index_map):
    return pl.BlockSpec(block_shape, index_map, pipeline_mode=pl.Buffered(1))


def _compiler_params():
    return pltpu.CompilerParams(
        dimension_semantics=("arbitrary", "arbitrary"),
        vmem_limit_bytes=VMEM_LIMIT_BYTES)


def _ada_kernel(c_ref, w_ref, b_ref, o_ref):
    c = c_ref[...]
    cond = (c * jax.nn.sigmoid(c)).astype(BF16)
    o_ref[...] = _dot(cond, w_ref[...].astype(BF16)) + b_ref[...]


def _ada_modulation(c, ada_w, ada_b):
    bsz, d = c.shape
    n = ada_w.shape[-1]
    out = pl.pallas_call(
        _ada_kernel,
        grid=(DEPTH, n // ADA_TILE),
        in_specs=[
            pl.BlockSpec((bsz, d), lambda l, j: (0, 0)),
            pl.BlockSpec((None, d, ADA_TILE), lambda l, j: (l, 0, j)),
            pl.BlockSpec((None, 1, ADA_TILE), lambda l, j: (l, 0, j)),
        ],
        out_specs=pl.BlockSpec((None, bsz, ADA_TILE), lambda l, j: (l, 0, j)),
        out_shape=jax.ShapeDtypeStruct((DEPTH, bsz, n), F32),
        compiler_params=_compiler_params(),
        name="adaln_mod",
    )(c, ada_w, ada_b.reshape(DEPTH, 1, n))
    return out.reshape(DEPTH, bsz, 6, d)


def _sconv_kernel(x_ref, mod_ref, g_ref, win_ref, cw_ref, wout_ref, o_ref,
                  hist_ref, y_ref):
    x = x_ref[...]
    t = x.shape[0]

    @pl.when(pl.program_id(1) == 0)
    def _():
        hist_ref[t:, :] = jnp.zeros((V7X_SUBLANES, D_MODEL), F32)

    mod = mod_ref[...]
    h = _norm_modulate(x, g_ref[...], mod[0:1], mod[1:2]).astype(BF16)
    for j in range(D_MODEL // SC_CHUNK):
        lo = j * SC_CHUNK
        gb = _dot(h, win_ref[:, lo:lo + SC_CHUNK])
        gc = _dot(h, win_ref[:, D_MODEL + lo:D_MODEL + lo + SC_CHUNK])
        u = _dot(h, win_ref[:, 2 * D_MODEL + lo:2 * D_MODEL + lo + SC_CHUNK])
        conv = _causal_conv3(gc * u, hist_ref, slice(lo, lo + SC_CHUNK),
                             cw_ref[:, lo:lo + SC_CHUNK])
        y_ref[:, lo:lo + SC_CHUNK] = (gb * conv).astype(BF16)
    out = _dot(y_ref[...], wout_ref[...])
    o_ref[...] = x + (1.0 + mod[2:3]) * out


def _sconv_layer(x, mods, layer, norm_g, w_in, conv_w, w_out, j):
    bsz, s, d = x.shape
    tm = SEQ_TILE
    return pl.pallas_call(
        _sconv_kernel,
        grid=(bsz, s // tm),
        in_specs=[
            pl.BlockSpec((None, tm, d), lambda b, i: (b, i, 0)),
            pl.BlockSpec((None, None, 6, d), lambda b, i: (layer, b, 0, 0)),
            pl.BlockSpec((None, 1, d), lambda b, i: (layer, 0, 0)),
            _resident((None, d, 3 * d), lambda b, i: (j, 0, 0)),
            pl.BlockSpec((None, CONV_WIDTH, d), lambda b, i: (j, 0, 0)),
            _resident((None, d, d), lambda b, i: (j, 0, 0)),
        ],
        out_specs=pl.BlockSpec((None, tm, d), lambda b, i: (b, i, 0)),
        out_shape=jax.ShapeDtypeStruct(x.shape, F32),
        scratch_shapes=[
            pltpu.VMEM((tm + V7X_SUBLANES, d), F32),
            pltpu.VMEM((tm, d), BF16),
        ],
        compiler_params=_compiler_params(),
        name=f"sconv_layer{layer}",
    )(x, mods, norm_g, w_in, conv_w, w_out)


def _ffn_kernel(x_ref, mod_ref, g_ref, wup_ref, cw_ref, cb_ref, wdown_ref,
                fg_ref, o_ref, hist_ref, a_ref, *, final_norm):
    x = x_ref[...]
    t = x.shape[0]

    @pl.when(pl.program_id(1) == 0)
    def _():
        hist_ref[t:, :] = jnp.zeros((V7X_SUBLANES, 2 * D_FF), F32)

    mod = mod_ref[...]
    h = _norm_modulate(x, g_ref[...], mod[3:4], mod[4:5]).astype(BF16)
    for j in range(D_FF // FF_CHUNK):
        halves = []
        for base in (0, D_FF):
            lo = base + j * FF_CHUNK
            cols = slice(lo, lo + FF_CHUNK)
            p = _dot(h, wup_ref[:, cols])
            halves.append(_causal_conv3(p, hist_ref, cols, cw_ref[:, cols])
                          + cb_ref[:, cols])
        gate, val = halves
        act = gate * jax.nn.sigmoid(gate) * val
        a_ref[:, j * FF_CHUNK:(j + 1) * FF_CHUNK] = act.astype(BF16)
    out = _dot(a_ref[...], wdown_ref[...])
    y = x + (1.0 + mod[5:6]) * out
    if final_norm:
        ms = jnp.mean(y * y, axis=-1, keepdims=True)
        y = y * lax.rsqrt(ms + EPS) * fg_ref[...]
    o_ref[...] = y


def _ffn_layer(x, mods, layer, norm_g, w_up, conv_w, conv_b, w_down, final_g,
               final_norm):
    bsz, s, d = x.shape
    tm = FFN_SEQ_TILE
    f2 = 2 * D_FF
    return pl.pallas_call(
        functools.partial(_ffn_kernel, final_norm=final_norm),
        grid=(bsz, s // tm),
        in_specs=[
            pl.BlockSpec((None, tm, d), lambda b, i: (b, i, 0)),
            pl.BlockSpec((None, None, 6, d), lambda b, i: (layer, b, 0, 0)),
            pl.BlockSpec((None, 1, d), lambda b, i: (layer, 0, 0)),
            _resident((None, d, f2), lambda b, i: (layer, 0, 0)),
            pl.BlockSpec((None, CONV_WIDTH, f2), lambda b, i: (layer, 0, 0)),
            pl.BlockSpec((None, 1, f2), lambda b, i: (layer, 0, 0)),
            _resident((None, D_FF, d), lambda b, i: (layer, 0, 0)),
            pl.BlockSpec((1, d), lambda b, i: (0, 0)),
        ],
        out_specs=pl.BlockSpec((None, tm, d), lambda b, i: (b, i, 0)),
        out_shape=jax.ShapeDtypeStruct(x.shape, F32),
        scratch_shapes=[
            pltpu.VMEM((tm + V7X_SUBLANES, f2), F32),
            pltpu.VMEM((tm, D_FF), BF16),
        ],
        compiler_params=_compiler_params(),
        name=f"ffn_layer{layer}",
    )(x, mods, norm_g, w_up, conv_w, conv_b.reshape(DEPTH, 1, f2), w_down,
      final_g.reshape(1, d))


ML_STATE_ROWS = ML_V_DIM + 16
ML_FM_ROWS = ML_QK_W + 2 * ML_V_W
ML_GATE_ROWS = 2 * V7X_SUBLANES


def _log_sigmoid(z):
    return jnp.minimum(z, 0.0) - jnp.log1p(jnp.exp(-jnp.abs(z)))


def _softcap(z):
    return GATE_SOFTCAP * jnp.tanh(z * (1.0 / GATE_SOFTCAP))


def _split3(z):
    hi = z.astype(BF16).astype(F32)
    r1 = z - hi
    mid = r1.astype(BF16).astype(F32)
    lo = (r1 - mid).astype(BF16).astype(F32)
    return hi, mid, lo


def _lane_tile(a, reps):
    return jnp.concatenate([a] * reps, axis=1)


def _dot_nt(a, b):
    return lax.dot_general(a, b, (((1,), (1,)), ((), ())),
                           preferred_element_type=F32)


def _dot_tn(a, b):
    return lax.dot_general(a, b, (((0,), (0,)), ((), ())),
                           preferred_element_type=F32)


def _mlstm_kernel(x_ref, mod_ref, g_ref, wfm_ref, wk_ref, wg_ref, wgt_ref,
                  gb_ref, gbt_ref, ng_ref, wout_ref, out_ref,
                  state_ref, m_ref, y_ref):
    @pl.when(pl.program_id(1) == 0)
    def _():
        state_ref[...] = jnp.zeros_like(state_ref)
        m_ref[...] = jnp.zeros_like(m_ref)

    x = x_ref[...]
    L = x.shape[0]
    reps = L // V7X_LANES
    mod = mod_ref[...]
    h = _norm_modulate(x, g_ref[...], mod[0:1], mod[1:2]).astype(BF16)

    row = lax.broadcasted_iota(jnp.int32, (L, L), 0)
    col = lax.broadcasted_iota(jnp.int32, (L, L), 1)
    lower = jnp.where(row >= col, 1.0, 0.0).astype(BF16)
    upper = jnp.where(row <= col, 1.0, 0.0).astype(BF16)
    key_before_query = row <= col

    capped = _softcap(_dot(h, wg_ref[...]) + gb_ref[...])
    f_tm = _log_sigmoid(pltpu.roll(capped, V7X_LANES - ML_HEADS, axis=1))
    hi, mid, lo = _split3(f_tm)
    r_tm = capped - (_dot(lower, hi.astype(BF16)) + _dot(lower, mid.astype(BF16))
                     + _dot(lower, lo.astype(BF16)))

    g_fm = _dot_nt(wgt_ref[...], h) + _lane_tile(gbt_ref[...], reps)
    i_fm = _softcap(g_fm[0:V7X_SUBLANES])
    f_fm = _log_sigmoid(_softcap(g_fm[V7X_SUBLANES:]))
    hi, mid, lo = _split3(f_fm)
    stacked = jnp.concatenate([hi, mid, lo, jnp.zeros_like(hi)], axis=0)
    cs = _dot(stacked.astype(BF16), upper)
    b_fm = cs[0:8] + cs[8:16] + cs[16:24]
    r_fm = i_fm - b_fm
    m_prev = _lane_tile(m_ref[...], reps)
    b_last = jnp.broadcast_to(b_fm[:, L - 1:L], (V7X_SUBLANES, L))
    inter = b_fm + m_prev
    m_new = b_last + jnp.maximum(m_prev, jnp.max(r_fm, axis=1, keepdims=True))
    decay = jnp.exp(b_last + m_prev - m_new)
    w_fm = jnp.exp(b_last + r_fm - m_new)
    m_ref[...] = m_new[:, 0:V7X_LANES]

    k_all = _dot(h, wk_ref[...]).astype(BF16)
    ng = ng_ref[...]

    for hd in range(ML_HEADS):
        q0 = hd * ML_QK_DIM
        v0 = ML_QK_W + hd * ML_V_DIM
        o0 = ML_QK_W + ML_V_W + hd * ML_V_DIM
        qt = (_dot_nt(wfm_ref[q0:q0 + ML_QK_DIM, :], h)
              * (ML_QK_DIM ** -0.5)).astype(BF16)
        vt = _dot_nt(wfm_ref[v0:v0 + ML_V_DIM, :], h).astype(BF16)
        ot = _dot_nt(wfm_ref[o0:o0 + ML_V_DIM, :], h)
        kh = k_all[:, q0:q0 + ML_QK_DIM]

        d_log = jnp.where(key_before_query,
                          r_tm[:, hd:hd + 1] + b_fm[hd:hd + 1, :], -jnp.inf)
        inter_h = inter[hd:hd + 1, :]
        m_t = jnp.maximum(inter_h, jnp.max(d_log, axis=0, keepdims=True))
        p = jnp.exp(d_log - m_t)
        sc = _dot(kh, qt) * p
        a = jnp.exp(inter_h - m_t)
        st_old = state_ref[hd]
        cq = _dot(st_old.astype(BF16), qt)
        num = a * cq[0:ML_V_DIM] + _dot(vt, sc.astype(BF16))
        den = a * cq[ML_V_DIM:ML_V_DIM + 1] + jnp.sum(sc, axis=0, keepdims=True)
        hc = num * (1.0 / jnp.maximum(jnp.abs(den), jnp.exp(-m_t)))

        w_row = w_fm[hd:hd + 1, :]
        n_rows = jnp.where(
            lax.broadcasted_iota(jnp.int32, (2 * V7X_SUBLANES, L), 0) == 0,
            w_row, 0.0)
        vw = jnp.concatenate([vt.astype(F32) * w_row, n_rows],
                             axis=0).astype(BF16)
        state_ref[hd] = decay[hd:hd + 1, 0:ML_QK_DIM] * st_old + _dot(vw, kh)

        hn = hc * lax.rsqrt(jnp.mean(hc * hc, axis=0, keepdims=True) + EPS)
        ng_h = _lane_tile(ng[hd * ML_V_DIM:(hd + 1) * ML_V_DIM, :], reps)
        y_ref[hd * ML_V_DIM:(hd + 1) * ML_V_DIM, :] = (
            hn * ng_h * jax.nn.sigmoid(ot)).astype(BF16)

    out = _dot_tn(y_ref[...], wout_ref[...])
    out_ref[...] = x + (1.0 + mod[2:3]) * out


def _mlstm_layer(x, mods, layer, norm_g, w_fm, w_k, w_g, w_gt, gate_b, gate_bt,
                 ng_fm, w_out, j):
    bsz, s, d = x.shape
    L = SEQ_TILE
    tok = lambda b, i: (b, i, 0)
    per_j = lambda b, i: (j, 0, 0)
    return pl.pallas_call(
        _mlstm_kernel,
        grid=(bsz, s // L),
        in_specs=[
            pl.BlockSpec((None, L, d), tok),
            pl.BlockSpec((None, None, 6, d), lambda b, i: (layer, b, 0, 0)),
            pl.BlockSpec((None, 1, d), lambda b, i: (layer, 0, 0)),
            _resident((None, ML_FM_ROWS, d), per_j),
            _resident((None, d, ML_QK_W), per_j),
            _resident((None, d, V7X_LANES), per_j),
            _resident((None, ML_GATE_ROWS, d), per_j),
            pl.BlockSpec((None, 1, V7X_LANES), per_j),
            pl.BlockSpec((None, ML_GATE_ROWS, V7X_LANES), per_j),
            _resident((None, ML_V_W, V7X_LANES), per_j),
            _resident((None, ML_V_W, d), per_j),
        ],
        out_specs=pl.BlockSpec((None, L, d), tok),
        out_shape=jax.ShapeDtypeStruct(x.shape, F32),
        scratch_shapes=[
            pltpu.VMEM((ML_HEADS, ML_STATE_ROWS, ML_QK_DIM), F32),
            pltpu.VMEM((V7X_SUBLANES, V7X_LANES), F32),
            pltpu.VMEM((ML_V_W, L), BF16),
        ],
        compiler_params=_compiler_params(),
        name=f"mlstm_layer{layer}",
    )(x, mods, norm_g, w_fm, w_k, w_g, w_gt, gate_b, gate_bt, ng_fm, w_out)


def _mlstm_weights(ml_w_in, ml_b_i, ml_b_f, ml_norm_g):
    n_ml = ml_w_in.shape[0]
    q, k, v, o, gi, gf = jnp.split(
        ml_w_in, [ML_QK_W, 2 * ML_QK_W, 2 * ML_QK_W + ML_V_W,
                  2 * ML_QK_W + 2 * ML_V_W, 2 * ML_QK_W + 2 * ML_V_W + ML_HEADS],
        axis=-1)
    w_fm = jnp.concatenate([q, v, o], axis=-1).transpose(0, 2, 1).astype(BF16)
    w_k = k.astype(BF16)
    lane_pad = V7X_LANES - 2 * ML_HEADS
    w_g = jnp.pad(jnp.concatenate([gi, gf], axis=-1),
                  ((0, 0), (0, 0), (0, lane_pad))).astype(BF16)
    row_pad = V7X_SUBLANES - ML_HEADS
    pad_rows = lambda t: jnp.pad(t, ((0, 0), (0, row_pad), (0, 0)))
    w_gt = jnp.concatenate([pad_rows(gi.transpose(0, 2, 1)),
                            pad_rows(gf.transpose(0, 2, 1))], axis=1).astype(BF16)
    gate_b = jnp.pad(jnp.concatenate([ml_b_i, ml_b_f], axis=-1),
                     ((0, 0), (0, lane_pad))).reshape(n_ml, 1, V7X_LANES)
    bt = jnp.concatenate([pad_rows(ml_b_i[:, :, None]),
                          pad_rows(ml_b_f[:, :, None])], axis=1)
    gate_bt = jnp.broadcast_to(bt, (n_ml, ML_GATE_ROWS, V7X_LANES))
    ng_fm = jnp.broadcast_to(ml_norm_g[:, :, None], (n_ml, ML_V_W, V7X_LANES))
    return w_fm, w_k, w_g, w_gt, gate_b, gate_bt, ng_fm


def kernel(x, c, ada_w, ada_b, norm_mix_g, norm_ffn_g, sc_w_in, sc_conv_w, sc_w_out, ml_w_in, ml_b_i, ml_b_f, ml_norm_g, ml_w_out, ffn_w_up, ffn_conv_w, ffn_conv_b, ffn_w_down, final_norm_g):
    mods = _ada_modulation(c, ada_w, ada_b)

    sc_w_in_b = sc_w_in.astype(BF16)
    sc_w_out_b = sc_w_out.astype(BF16)
    ml_weights = _mlstm_weights(ml_w_in, ml_b_i, ml_b_f, ml_norm_g)
    ml_w_out_b = ml_w_out.astype(BF16)
    ffn_w_up_b = ffn_w_up.astype(BF16)
    ffn_w_down_b = ffn_w_down.astype(BF16)
    norm_mix_g = norm_mix_g.reshape(DEPTH, 1, D_MODEL)
    norm_ffn_g = norm_ffn_g.reshape(DEPTH, 1, D_MODEL)

    for layer in range(DEPTH):
        j = layer // 2
        if layer % 2 == 0:
            x = _sconv_layer(x, mods, layer, norm_mix_g, sc_w_in_b, sc_conv_w,
                             sc_w_out_b, j)
        else:
            x = _mlstm_layer(x, mods, layer, norm_mix_g, *ml_weights,
                             ml_w_out_b, j)
        x = _ffn_layer(x, mods, layer, norm_ffn_g, ffn_w_up_b, ffn_conv_w,
                       ffn_conv_b, ffn_w_down_b, final_norm_g,
                       final_norm=(layer == DEPTH - 1))
    return x
```

```python
import functools

import jax
import jax.numpy as jnp
from jax import lax
from jax.experimental import pallas as pl
from jax.experimental.pallas import tpu as pltpu

D_MODEL = 1024
DEPTH = 4
CONV_WIDTH = 3
ML_HEADS = 4
ML_V_DIM = D_MODEL // ML_HEADS
ML_QK_DIM = ML_V_DIM // 2
ML_QK_W = ML_HEADS * ML_QK_DIM
ML_V_W = ML_HEADS * ML_V_DIM
GATE_SOFTCAP = 15.0
D_FF = 2816
EPS = 1e-6

V7X_LANES = 128
V7X_SUBLANES = 8
V7X_MXU_DIM = 256
V7X_VMEM_BYTES = 64 * 1024 * 1024
VMEM_LIMIT_BYTES = V7X_VMEM_BYTES - 8 * 1024 * 1024

SEQ_TILE = 256
FFN_SEQ_TILE = 512
SCONV_SEQ_TILE = 512
STREAM_ROWS = 256
FF_CHUNK = V7X_MXU_DIM
SC_CHUNK = V7X_MXU_DIM
ADA_TILE = 1536

BF16 = jnp.bfloat16
F32 = jnp.float32


def _dot(a, b):
    return jnp.dot(a, b, preferred_element_type=F32)


def _norm_modulate(x, g, shift, scale):
    ms = jnp.mean(x * x, axis=-1, keepdims=True)
    y = x * lax.rsqrt(ms + EPS)
    return (y * g) * (1.0 + scale) + shift


def _carry_halo(hist_ref, t):
    halo = V7X_SUBLANES
    hist_ref[0:halo, :] = hist_ref[t:t + halo, :]


def _causal_conv3(p, hist_ref, r0, cols, w):
    n = p.shape[0]
    base = V7X_SUBLANES + r0
    hist_ref[base:base + n, cols] = p
    p1 = hist_ref[base - 1:base - 1 + n, cols]
    p2 = hist_ref[base - 2:base - 2 + n, cols]
    return w[0:1] * p2 + w[1:2] * p1 + w[2:3] * p


def _resident(block_shape, index_map):
    return pl.BlockSpec(block_shape, index_map, pipeline_mode=pl.Buffered(1))


def _compiler_params():
    return pltpu.CompilerParams(
        dimension_semantics=("arbitrary", "arbitrary"),
        vmem_limit_bytes=VMEM_LIMIT_BYTES)


def _ada_kernel(c_ref, w_ref, b_ref, o_ref):
    c = c_ref[...]
    cond = (c * jax.nn.sigmoid(c)).astype(BF16)
    o_ref[...] = _dot(cond, w_ref[...].astype(BF16)) + b_ref[...]


def _ada_modulation(c, ada_w, ada_b):
    bsz, d = c.shape
    n = ada_w.shape[-1]
    out = pl.pallas_call(
        _ada_kernel,
        grid=(DEPTH, n // ADA_TILE),
        in_specs=[
            pl.BlockSpec((bsz, d), lambda l, j: (0, 0)),
            pl.BlockSpec((None, d, ADA_TILE), lambda l, j: (l, 0, j)),
            pl.BlockSpec((None, 1, ADA_TILE), lambda l, j: (l, 0, j)),
        ],
        out_specs=pl.BlockSpec((None, bsz, ADA_TILE), lambda l, j: (l, 0, j)),
        out_shape=jax.ShapeDtypeStruct((DEPTH, bsz, n), F32),
        compiler_params=_compiler_params(),
        name="adaln_mod",
    )(c, ada_w, ada_b.reshape(DEPTH, 1, n))
    return out.reshape(DEPTH, bsz, 6, d)


def _sconv_kernel(x_ref, mod_ref, g_ref, win_ref, cw_ref, wout_ref, o_ref,
                  hist_ref, y_ref):
    t = x_ref.shape[0]

    @pl.when(pl.program_id(1) == 0)
    def _():
        hist_ref[t:, :] = jnp.zeros((V7X_SUBLANES, D_MODEL), F32)

    _carry_halo(hist_ref, t)
    mod = mod_ref[...]
    for r0 in range(0, t, STREAM_ROWS):
        rows = slice(r0, r0 + STREAM_ROWS)
        xs = x_ref[rows, :]
        h = _norm_modulate(xs, g_ref[...], mod[0:1], mod[1:2]).astype(BF16)
        for j in range(D_MODEL // SC_CHUNK):
            lo = j * SC_CHUNK
            gb = _dot(h, win_ref[:, lo:lo + SC_CHUNK])
            gc = _dot(h, win_ref[:, D_MODEL + lo:D_MODEL + lo + SC_CHUNK])
            u = _dot(h, win_ref[:, 2 * D_MODEL + lo:2 * D_MODEL + lo + SC_CHUNK])
            conv = _causal_conv3(gc * u, hist_ref, r0, slice(lo, lo + SC_CHUNK),
                                 cw_ref[:, lo:lo + SC_CHUNK])
            y_ref[rows, lo:lo + SC_CHUNK] = (gb * conv).astype(BF16)
        out = _dot(y_ref[rows, :], wout_ref[...])
        o_ref[rows, :] = xs + (1.0 + mod[2:3]) * out


def _sconv_layer(x, mods, layer, norm_g, w_in, conv_w, w_out, j):
    bsz, s, d = x.shape
    tm = SCONV_SEQ_TILE
    return pl.pallas_call(
        _sconv_kernel,
        grid=(bsz, s // tm),
        in_specs=[
            pl.BlockSpec((None, tm, d), lambda b, i: (b, i, 0)),
            pl.BlockSpec((None, None, 6, d), lambda b, i: (layer, b, 0, 0)),
            pl.BlockSpec((None, 1, d), lambda b, i: (layer, 0, 0)),
            _resident((None, d, 3 * d), lambda b, i: (j, 0, 0)),
            pl.BlockSpec((None, CONV_WIDTH, d), lambda b, i: (j, 0, 0)),
            _resident((None, d, d), lambda b, i: (j, 0, 0)),
        ],
        out_specs=pl.BlockSpec((None, tm, d), lambda b, i: (b, i, 0)),
        out_shape=jax.ShapeDtypeStruct(x.shape, F32),
        scratch_shapes=[
            pltpu.VMEM((tm + V7X_SUBLANES, d), F32),
            pltpu.VMEM((tm, d), BF16),
        ],
        compiler_params=_compiler_params(),
        name=f"sconv_layer{layer}",
    )(x, mods, norm_g, w_in, conv_w, w_out)


def _ffn_kernel(x_ref, mod_ref, g_ref, wup_ref, cw_ref, cb_ref, wdown_ref,
                fg_ref, o_ref, hist_ref, a_ref, *, final_norm):
    t = x_ref.shape[0]

    @pl.when(pl.program_id(1) == 0)
    def _():
        hist_ref[t:, :] = jnp.zeros((V7X_SUBLANES, 2 * D_FF), F32)

    _carry_halo(hist_ref, t)
    mod = mod_ref[...]
    for r0 in range(0, t, STREAM_ROWS):
        rows = slice(r0, r0 + STREAM_ROWS)
        xs = x_ref[rows, :]
        h = _norm_modulate(xs, g_ref[...], mod[3:4], mod[4:5]).astype(BF16)
        for j in range(D_FF // FF_CHUNK):
            halves = []
            for base in (0, D_FF):
                lo = base + j * FF_CHUNK
                cols = slice(lo, lo + FF_CHUNK)
                p = _dot(h, wup_ref[:, cols])
                halves.append(
                    _causal_conv3(p, hist_ref, r0, cols, cw_ref[:, cols])
                    + cb_ref[:, cols])
            gate, val = halves
            act = gate * jax.nn.sigmoid(gate) * val
            a_ref[rows, j * FF_CHUNK:(j + 1) * FF_CHUNK] = act.astype(BF16)
        out = _dot(a_ref[rows, :], wdown_ref[...])
        y = xs + (1.0 + mod[5:6]) * out
        if final_norm:
            ms = jnp.mean(y * y, axis=-1, keepdims=True)
            y = y * lax.rsqrt(ms + EPS) * fg_ref[...]
        o_ref[rows, :] = y


def _ffn_layer(x, mods, layer, norm_g, w_up, conv_w, conv_b, w_down, final_g,
               final_norm):
    bsz, s, d = x.shape
    tm = FFN_SEQ_TILE
    f2 = 2 * D_FF
    return pl.pallas_call(
        functools.partial(_ffn_kernel, final_norm=final_norm),
        grid=(bsz, s // tm),
        in_specs=[
            pl.BlockSpec((None, tm, d), lambda b, i: (b, i, 0)),
            pl.BlockSpec((None, None, 6, d), lambda b, i: (layer, b, 0, 0)),
            pl.BlockSpec((None, 1, d), lambda b, i: (layer, 0, 0)),
            _resident((None, d, f2), lambda b, i: (layer, 0, 0)),
            pl.BlockSpec((None, CONV_WIDTH, f2), lambda b, i: (layer, 0, 0)),
            pl.BlockSpec((None, 1, f2), lambda b, i: (layer, 0, 0)),
            _resident((None, D_FF, d), lambda b, i: (layer, 0, 0)),
            pl.BlockSpec((1, d), lambda b, i: (0, 0)),
        ],
        out_specs=pl.BlockSpec((None, tm, d), lambda b, i: (b, i, 0)),
        out_shape=jax.ShapeDtypeStruct(x.shape, F32),
        scratch_shapes=[
            pltpu.VMEM((tm + V7X_SUBLANES, f2), F32),
            pltpu.VMEM((tm, D_FF), BF16),
        ],
        compiler_params=_compiler_params(),
        name=f"ffn_layer{layer}",
    )(x, mods, norm_g, w_up, conv_w, conv_b.reshape(DEPTH, 1, f2), w_down,
      final_g.reshape(1, d))


ML_STATE_ROWS = ML_V_DIM + 16
ML_FM_ROWS = ML_QK_W + 2 * ML_V_W
ML_GATE_ROWS = 2 * V7X_SUBLANES


def _log_sigmoid(z):
    return jnp.minimum(z, 0.0) - jnp.log1p(jnp.exp(-jnp.abs(z)))


def _softcap(z):
    return GATE_SOFTCAP * jnp.tanh(z * (1.0 / GATE_SOFTCAP))


def _split3(z):
    hi = z.astype(BF16).astype(F32)
    r1 = z - hi
    mid = r1.astype(BF16).astype(F32)
    lo = (r1 - mid).astype(BF16).astype(F32)
    return hi, mid, lo


def _lane_tile(a, reps):
    return jnp.concatenate([a] * reps, axis=1)


def _dot_nt(a, b):
    return lax.dot_general(a, b, (((1,), (1,)), ((), ())),
                           preferred_element_type=F32)


def _dot_tn(a, b):
    return lax.dot_general(a, b, (((0,), (0,)), ((), ())),
                           preferred_element_type=F32)


def _mlstm_kernel(x_ref, mod_ref, g_ref, wfm_ref, wk_ref, wg_ref, wgt_ref,
                  gb_ref, gbt_ref, ng_ref, wout_ref, out_ref,
                  state_ref, m_ref, y_ref):
    @pl.when(pl.program_id(1) == 0)
    def _():
        state_ref[...] = jnp.zeros_like(state_ref)
        m_ref[...] = jnp.zeros_like(m_ref)

    x = x_ref[...]
    L = x.shape[0]
    reps = L // V7X_LANES
    mod = mod_ref[...]
    h = _norm_modulate(x, g_ref[...], mod[0:1], mod[1:2]).astype(BF16)

    row = lax.broadcasted_iota(jnp.int32, (L, L), 0)
    col = lax.broadcasted_iota(jnp.int32, (L, L), 1)
    lower = jnp.where(row >= col, 1.0, 0.0).astype(BF16)
    upper = jnp.where(row <= col, 1.0, 0.0).astype(BF16)
    key_before_query = row <= col

    capped = _softcap(_dot(h, wg_ref[...]) + gb_ref[...])
    f_tm = _log_sigmoid(pltpu.roll(capped, V7X_LANES - ML_HEADS, axis=1))
    hi, mid, lo = _split3(f_tm)
    r_tm = capped - (_dot(lower, hi.astype(BF16)) + _dot(lower, mid.astype(BF16))
                     + _dot(lower, lo.astype(BF16)))

    g_fm = _dot_nt(wgt_ref[...], h) + _lane_tile(gbt_ref[...], reps)
    i_fm = _softcap(g_fm[0:V7X_SUBLANES])
    f_fm = _log_sigmoid(_softcap(g_fm[V7X_SUBLANES:]))
    hi, mid, lo = _split3(f_fm)
    stacked = jnp.concatenate([hi, mid, lo, jnp.zeros_like(hi)], axis=0)
    cs = _dot(stacked.astype(BF16), upper)
    b_fm = cs[0:8] + cs[8:16] + cs[16:24]
    r_fm = i_fm - b_fm
    m_prev = _lane_tile(m_ref[...], reps)
    b_last = jnp.broadcast_to(b_fm[:, L - 1:L], (V7X_SUBLANES, L))
    inter = b_fm + m_prev
    m_new = b_last + jnp.maximum(m_prev, jnp.max(r_fm, axis=1, keepdims=True))
    decay = jnp.exp(b_last + m_prev - m_new)
    w_fm = jnp.exp(b_last + r_fm - m_new)
    m_ref[...] = m_new[:, 0:V7X_LANES]

    k_all = _dot(h, wk_ref[...]).astype(BF16)
    ng = ng_ref[...]

    for hd in range(ML_HEADS):
        q0 = hd * ML_QK_DIM
        v0 = ML_QK_W + hd * ML_V_DIM
        o0 = ML_QK_W + ML_V_W + hd * ML_V_DIM
        qt = (_dot_nt(wfm_ref[q0:q0 + ML_QK_DIM, :], h)
              * (ML_QK_DIM ** -0.5)).astype(BF16)
        vt = _dot_nt(wfm_ref[v0:v0 + ML_V_DIM, :], h).astype(BF16)
        ot = _dot_nt(wfm_ref[o0:o0 + ML_V_DIM, :], h)
        kh = k_all[:, q0:q0 + ML_QK_DIM]

        d_log = jnp.where(key_before_query,
                          r_tm[:, hd:hd + 1] + b_fm[hd:hd + 1, :], -jnp.inf)
        inter_h = inter[hd:hd + 1, :]
        m_t = jnp.maximum(inter_h, jnp.max(d_log, axis=0, keepdims=True))
        p = jnp.exp(d_log - m_t)
        sc = _dot(kh, qt) * p
        a = jnp.exp(inter_h - m_t)
        st_old = state_ref[hd]
        cq = _dot(st_old.astype(BF16), qt)
        num = a * cq[0:ML_V_DIM] + _dot(vt, sc.astype(BF16))
        den = a * cq[ML_V_DIM:ML_V_DIM + 1] + jnp.sum(sc, axis=0, keepdims=True)
        hc = num * (1.0 / jnp.maximum(jnp.abs(den), jnp.exp(-m_t)))

        w_row = w_fm[hd:hd + 1, :]
        n_rows = jnp.where(
            lax.broadcasted_iota(jnp.int32, (2 * V7X_SUBLANES, L), 0) == 0,
            w_row, 0.0)
        vw = jnp.concatenate([vt.astype(F32) * w_row, n_rows],
                             axis=0).astype(BF16)
        state_ref[hd] = decay[hd:hd + 1, 0:ML_QK_DIM] * st_old + _dot(vw, kh)

        hn = hc * lax.rsqrt(jnp.mean(hc * hc, axis=0, keepdims=True) + EPS)
        ng_h = _lane_tile(ng[hd * ML_V_DIM:(hd + 1) * ML_V_DIM, :], reps)
        y_ref[hd * ML_V_DIM:(hd + 1) * ML_V_DIM, :] = (
            hn * ng_h * jax.nn.sigmoid(ot)).astype(BF16)

    out = _dot_tn(y_ref[...], wout_ref[...])
    out_ref[...] = x + (1.0 + mod[2:3]) * out


def _mlstm_layer(x, mods, layer, norm_g, w_fm, w_k, w_g, w_gt, gate_b, gate_bt,
                 ng_fm, w_out, j):
    bsz, s, d = x.shape
    L = SEQ_TILE
    tok = lambda b, i: (b, i, 0)
    per_j = lambda b, i: (j, 0, 0)
    return pl.pallas_call(
        _mlstm_kernel,
        grid=(bsz, s // L),
        in_specs=[
            pl.BlockSpec((None, L, d), tok),
            pl.BlockSpec((None, None, 6, d), lambda b, i: (layer, b, 0, 0)),
            pl.BlockSpec((None, 1, d), lambda b, i: (layer, 0, 0)),
            _resident((None, ML_FM_ROWS, d), per_j),
            _resident((None, d, ML_QK_W), per_j),
            _resident((None, d, V7X_LANES), per_j),
            _resident((None, ML_GATE_ROWS, d), per_j),
            pl.BlockSpec((None, 1, V7X_LANES), per_j),
            pl.BlockSpec((None, ML_GATE_ROWS, V7X_LANES), per_j),
            _resident((None, ML_V_W, V7X_LANES), per_j),
            _resident((None, ML_V_W, d), per_j),
        ],
        out_specs=pl.BlockSpec((None, L, d), tok),
        out_shape=jax.ShapeDtypeStruct(x.shape, F32),
        scratch_shapes=[
            pltpu.VMEM((ML_HEADS, ML_STATE_ROWS, ML_QK_DIM), F32),
            pltpu.VMEM((V7X_SUBLANES, V7X_LANES), F32),
            pltpu.VMEM((ML_V_W, L), BF16),
        ],
        compiler_params=_compiler_params(),
        name=f"mlstm_layer{layer}",
    )(x, mods, norm_g, w_fm, w_k, w_g, w_gt, gate_b, gate_bt, ng_fm, w_out)


def _mlstm_weights(ml_w_in, ml_b_i, ml_b_f, ml_norm_g):
    n_ml = ml_w_in.shape[0]
    q, k, v, o, gi, gf = jnp.split(
        ml_w_in, [ML_QK_W, 2 * ML_QK_W, 2 * ML_QK_W + ML_V_W,
                  2 * ML_QK_W + 2 * ML_V_W, 2 * ML_QK_W + 2 * ML_V_W + ML_HEADS],
        axis=-1)
    w_fm = jnp.concatenate([q, v, o], axis=-1).transpose(0, 2, 1).astype(BF16)
    w_k = k.astype(BF16)
    lane_pad = V7X_LANES - 2 * ML_HEADS
    w_g = jnp.pad(jnp.concatenate([gi, gf], axis=-1),
                  ((0, 0), (0, 0), (0, lane_pad))).astype(BF16)
    row_pad = V7X_SUBLANES - ML_HEADS
    pad_rows = lambda t: jnp.pad(t, ((0, 0), (0, row_pad), (0, 0)))
    w_gt = jnp.concatenate([pad_rows(gi.transpose(0, 2, 1)),
                            pad_rows(gf.transpose(0, 2, 1))], axis=1).astype(BF16)
    gate_b = jnp.pad(jnp.concatenate([ml_b_i, ml_b_f], axis=-1),
                     ((0, 0), (0, lane_pad))).reshape(n_ml, 1, V7X_LANES)
    bt = jnp.concatenate([pad_rows(ml_b_i[:, :, None]),
                          pad_rows(ml_b_f[:, :, None])], axis=1)
    gate_bt = jnp.broadcast_to(bt, (n_ml, ML_GATE_ROWS, V7X_LANES))
    ng_fm = jnp.broadcast_to(ml_norm_g[:, :, None], (n_ml, ML_V_W, V7X_LANES))
    return w_fm, w_k, w_g, w_gt, gate_b, gate_bt, ng_fm


def kernel(x, c, ada_w, ada_b, norm_mix_g, norm_ffn_g, sc_w_in, sc_conv_w, sc_w_out, ml_w_in, ml_b_i, ml_b_f, ml_norm_g, ml_w_out, ffn_w_up, ffn_conv_w, ffn_conv_b, ffn_w_down, final_norm_g):
    mods = _ada_modulation(c, ada_w, ada_b)

    sc_w_in_b = sc_w_in.astype(BF16)
    sc_w_out_b = sc_w_out.astype(BF16)
    ml_weights = _mlstm_weights(ml_w_in, ml_b_i, ml_b_f, ml_norm_g)
    ml_w_out_b = ml_w_out.astype(BF16)
    ffn_w_up_b = ffn_w_up.astype(BF16)
    ffn_w_down_b = ffn_w_down.astype(BF16)
    norm_mix_g = norm_mix_g.reshape(DEPTH, 1, D_MODEL)
    norm_ffn_g = norm_ffn_g.reshape(DEPTH, 1, D_MODEL)

    for layer in range(DEPTH):
        j = layer // 2
        if layer % 2 == 0:
            x = _sconv_layer(x, mods, layer, norm_mix_g, sc_w_in_b, sc_conv_w,
                             sc_w_out_b, j)
        else:
            x = _mlstm_layer(x, mods, layer, norm_mix_g, *ml_weights,
                             ml_w_out_b, j)
        x = _ffn_layer(x, mods, layer, norm_ffn_g, ffn_w_up_b, ffn_conv_w,
                       ffn_conv_b, ffn_w_down_b, final_norm_g,
                       final_norm=(layer == DEPTH - 1))
    return x
```

```python
import functools

import jax
import jax.numpy as jnp
from jax import lax
from jax.experimental import pallas as pl
from jax.experimental.pallas import tpu as pltpu

D_MODEL = 1024
DEPTH = 4
CONV_WIDTH = 3
ML_HEADS = 4
ML_V_DIM = D_MODEL // ML_HEADS
ML_QK_DIM = ML_V_DIM // 2
ML_QK_W = ML_HEADS * ML_QK_DIM
ML_V_W = ML_HEADS * ML_V_DIM
GATE_SOFTCAP = 15.0
D_FF = 2816
EPS = 1e-6

V7X_LANES = 128
V7X_SUBLANES = 8
V7X_MXU_DIM = 256
V7X_VMEM_BYTES = 64 * 1024 * 1024
VMEM_LIMIT_BYTES = V7X_VMEM_BYTES - 8 * 1024 * 1024

ML_CHUNK = 256
ML_SEQ_TILE = 512
FFN_SEQ_TILE = 512
SCONV_SEQ_TILE = 512
SCONV_STREAM_ROWS = 256
FFN_STREAM_ROWS = 512
FF_CHUNK = V7X_MXU_DIM
SC_CHUNK = V7X_MXU_DIM
ADA_TILE = 1536

BF16 = jnp.bfloat16
F32 = jnp.float32


def _dot(a, b):
    return jnp.dot(a, b, preferred_element_type=F32)


def _norm_modulate(x, g, shift, scale):
    ms = jnp.mean(x * x, axis=-1, keepdims=True)
    y = x * lax.rsqrt(ms + EPS)
    return (y * g) * (1.0 + scale) + shift


def _carry_halo(hist_ref, t):
    halo = V7X_SUBLANES
    hist_ref[0:halo, :] = hist_ref[t:t + halo, :]


def _causal_conv3(p, hist_ref, r0, cols, w):
    n = p.shape[0]
    base = V7X_SUBLANES + r0
    hist_ref[base:base + n, cols] = p
    p1 = hist_ref[base - 1:base - 1 + n, cols]
    p2 = hist_ref[base - 2:base - 2 + n, cols]
    return w[0:1] * p2 + w[1:2] * p1 + w[2:3] * p


def _resident(block_shape, index_map):
    return pl.BlockSpec(block_shape, index_map, pipeline_mode=pl.Buffered(1))


def _compiler_params():
    return pltpu.CompilerParams(
        dimension_semantics=("arbitrary", "arbitrary"),
        vmem_limit_bytes=VMEM_LIMIT_BYTES)


def _ada_kernel(c_ref, w_ref, b_ref, o_ref):
    c = c_ref[...]
    cond = (c * jax.nn.sigmoid(c)).astype(BF16)
    o_ref[...] = _dot(cond, w_ref[...].astype(BF16)) + b_ref[...]


def _ada_modulation(c, ada_w, ada_b):
    bsz, d = c.shape
    n = ada_w.shape[-1]
    out = pl.pallas_call(
        _ada_kernel,
        grid=(DEPTH, n // ADA_TILE),
        in_specs=[
            pl.BlockSpec((bsz, d), lambda l, j: (0, 0)),
            pl.BlockSpec((None, d, ADA_TILE), lambda l, j: (l, 0, j)),
            pl.BlockSpec((None, 1, ADA_TILE), lambda l, j: (l, 0, j)),
        ],
        out_specs=pl.BlockSpec((None, bsz, ADA_TILE), lambda l, j: (l, 0, j)),
        out_shape=jax.ShapeDtypeStruct((DEPTH, bsz, n), F32),
        compiler_params=_compiler_params(),
        name="adaln_mod",
    )(c, ada_w, ada_b.reshape(DEPTH, 1, n))
    return out.reshape(DEPTH, bsz, 6, d)


def _sconv_kernel(x_ref, mod_ref, g_ref, win_ref, cw_ref, wout_ref, o_ref,
                  hist_ref, y_ref):
    t = x_ref.shape[0]

    @pl.when(pl.program_id(1) == 0)
    def _():
        hist_ref[t:, :] = jnp.zeros((V7X_SUBLANES, D_MODEL), F32)

    _carry_halo(hist_ref, t)
    mod = mod_ref[...]
    for r0 in range(0, t, SCONV_STREAM_ROWS):
        rows = slice(r0, r0 + SCONV_STREAM_ROWS)
        xs = x_ref[rows, :]
        h = _norm_modulate(xs, g_ref[...], mod[0:1], mod[1:2]).astype(BF16)
        for j in range(D_MODEL // SC_CHUNK):
            lo = j * SC_CHUNK
            gb = _dot(h, win_ref[:, lo:lo + SC_CHUNK])
            gc = _dot(h, win_ref[:, D_MODEL + lo:D_MODEL + lo + SC_CHUNK])
            u = _dot(h, win_ref[:, 2 * D_MODEL + lo:2 * D_MODEL + lo + SC_CHUNK])
            conv = _causal_conv3(gc * u, hist_ref, r0, slice(lo, lo + SC_CHUNK),
                                 cw_ref[:, lo:lo + SC_CHUNK])
            y_ref[rows, lo:lo + SC_CHUNK] = (gb * conv).astype(BF16)
        out = _dot(y_ref[rows, :], wout_ref[...])
        o_ref[rows, :] = xs + (1.0 + mod[2:3]) * out


def _sconv_layer(x, mods, layer, norm_g, w_in, conv_w, w_out, j):
    bsz, s, d = x.shape
    tm = SCONV_SEQ_TILE
    return pl.pallas_call(
        _sconv_kernel,
        grid=(bsz, s // tm),
        in_specs=[
            pl.BlockSpec((None, tm, d), lambda b, i: (b, i, 0)),
            pl.BlockSpec((None, None, 6, d), lambda b, i: (layer, b, 0, 0)),
            pl.BlockSpec((None, 1, d), lambda b, i: (layer, 0, 0)),
            _resident((None, d, 3 * d), lambda b, i: (j, 0, 0)),
            pl.BlockSpec((None, CONV_WIDTH, d), lambda b, i: (j, 0, 0)),
            _resident((None, d, d), lambda b, i: (j, 0, 0)),
        ],
        out_specs=pl.BlockSpec((None, tm, d), lambda b, i: (b, i, 0)),
        out_shape=jax.ShapeDtypeStruct(x.shape, F32),
        scratch_shapes=[
            pltpu.VMEM((tm + V7X_SUBLANES, d), F32),
            pltpu.VMEM((tm, d), BF16),
        ],
        compiler_params=_compiler_params(),
        name=f"sconv_layer{layer}",
    )(x, mods, norm_g, w_in, conv_w, w_out)


def _ffn_kernel(x_ref, mod_ref, g_ref, wup_ref, cw_ref, cb_ref, wdown_ref,
                fg_ref, o_ref, hist_ref, a_ref, *, final_norm):
    t = x_ref.shape[0]

    @pl.when(pl.program_id(1) == 0)
    def _():
        hist_ref[t:, :] = jnp.zeros((V7X_SUBLANES, 2 * D_FF), F32)

    _carry_halo(hist_ref, t)
    mod = mod_ref[...]
    for r0 in range(0, t, FFN_STREAM_ROWS):
        rows = slice(r0, r0 + FFN_STREAM_ROWS)
        xs = x_ref[rows, :]
        h = _norm_modulate(xs, g_ref[...], mod[3:4], mod[4:5]).astype(BF16)
        for j in range(D_FF // FF_CHUNK):
            halves = []
            for base in (0, D_FF):
                lo = base + j * FF_CHUNK
                cols = slice(lo, lo + FF_CHUNK)
                p = _dot(h, wup_ref[:, cols])
                halves.append(
                    _causal_conv3(p, hist_ref, r0, cols, cw_ref[:, cols])
                    + cb_ref[:, cols])
            gate, val = halves
            act = gate * jax.nn.sigmoid(gate) * val
            a_ref[rows, j * FF_CHUNK:(j + 1) * FF_CHUNK] = act.astype(BF16)
        out = _dot(a_ref[rows, :], wdown_ref[...])
        y = xs + (1.0 + mod[5:6]) * out
        if final_norm:
            ms = jnp.mean(y * y, axis=-1, keepdims=True)
            y = y * lax.rsqrt(ms + EPS) * fg_ref[...]
        o_ref[rows, :] = y


def _ffn_layer(x, mods, layer, norm_g, w_up, conv_w, conv_b, w_down, final_g,
               final_norm):
    bsz, s, d = x.shape
    tm = FFN_SEQ_TILE
    f2 = 2 * D_FF
    return pl.pallas_call(
        functools.partial(_ffn_kernel, final_norm=final_norm),
        grid=(bsz, s // tm),
        in_specs=[
            pl.BlockSpec((None, tm, d), lambda b, i: (b, i, 0)),
            pl.BlockSpec((None, None, 6, d), lambda b, i: (layer, b, 0, 0)),
            pl.BlockSpec((None, 1, d), lambda b, i: (layer, 0, 0)),
            _resident((None, d, f2), lambda b, i: (layer, 0, 0)),
            pl.BlockSpec((None, CONV_WIDTH, f2), lambda b, i: (layer, 0, 0)),
            pl.BlockSpec((None, 1, f2), lambda b, i: (layer, 0, 0)),
            _resident((None, D_FF, d), lambda b, i: (layer, 0, 0)),
            pl.BlockSpec((1, d), lambda b, i: (0, 0)),
        ],
        out_specs=pl.BlockSpec((None, tm, d), lambda b, i: (b, i, 0)),
        out_shape=jax.ShapeDtypeStruct(x.shape, F32),
        scratch_shapes=[
            pltpu.VMEM((tm + V7X_SUBLANES, f2), F32),
            pltpu.VMEM((tm, D_FF), BF16),
        ],
        compiler_params=_compiler_params(),
        name=f"ffn_layer{layer}",
    )(x, mods, norm_g, w_up, conv_w, conv_b.reshape(DEPTH, 1, f2), w_down,
      final_g.reshape(1, d))


ML_STATE_ROWS = ML_V_DIM + 16
ML_FM_ROWS = ML_QK_W + 2 * ML_V_W
ML_GATE_ROWS = 2 * V7X_SUBLANES


def _log_sigmoid(z):
    return jnp.minimum(z, 0.0) - jnp.log1p(jnp.exp(-jnp.abs(z)))


def _softcap(z):
    return GATE_SOFTCAP * jnp.tanh(z * (1.0 / GATE_SOFTCAP))


def _split3(z):
    hi = z.astype(BF16).astype(F32)
    r1 = z - hi
    mid = r1.astype(BF16).astype(F32)
    lo = (r1 - mid).astype(BF16).astype(F32)
    return hi, mid, lo


def _lane_tile(a, reps):
    return jnp.concatenate([a] * reps, axis=1)


def _dot_nt(a, b):
    return lax.dot_general(a, b, (((1,), (1,)), ((), ())),
                           preferred_element_type=F32)


def _dot_tn(a, b):
    return lax.dot_general(a, b, (((0,), (0,)), ((), ())),
                           preferred_element_type=F32)


def _mlstm_kernel(x_ref, mod_ref, g_ref, wfm_ref, wk_ref, wg_ref, wgt_ref,
                  gb_ref, gbt_ref, ng_ref, wout_ref, out_ref,
                  state_ref, m_ref, y_ref):
    @pl.when(pl.program_id(1) == 0)
    def _():
        state_ref[...] = jnp.zeros_like(state_ref)
        m_ref[...] = jnp.zeros_like(m_ref)

    for c in range(x_ref.shape[0] // ML_CHUNK):
        rows = slice(c * ML_CHUNK, (c + 1) * ML_CHUNK)
        out_ref[rows, :] = _mlstm_chunk(
            x_ref[rows, :], mod_ref[...], g_ref, wfm_ref, wk_ref, wg_ref,
            wgt_ref, gb_ref, gbt_ref, ng_ref, wout_ref, state_ref, m_ref,
            y_ref.at[c])


def _mlstm_chunk(x, mod, g_ref, wfm_ref, wk_ref, wg_ref, wgt_ref, gb_ref,
                 gbt_ref, ng_ref, wout_ref, state_ref, m_ref, y_ref):
    L = x.shape[0]
    reps = L // V7X_LANES
    h = _norm_modulate(x, g_ref[...], mod[0:1], mod[1:2]).astype(BF16)

    row = lax.broadcasted_iota(jnp.int32, (L, L), 0)
    col = lax.broadcasted_iota(jnp.int32, (L, L), 1)
    lower = jnp.where(row >= col, 1.0, 0.0).astype(BF16)
    upper = jnp.where(row <= col, 1.0, 0.0).astype(BF16)
    key_before_query = row <= col

    capped = _softcap(_dot(h, wg_ref[...]) + gb_ref[...])
    f_tm = _log_sigmoid(pltpu.roll(capped, V7X_LANES - ML_HEADS, axis=1))
    hi, mid, lo = _split3(f_tm)
    r_tm = capped - (_dot(lower, hi.astype(BF16)) + _dot(lower, mid.astype(BF16))
                     + _dot(lower, lo.astype(BF16)))

    g_fm = _dot_nt(wgt_ref[...], h) + _lane_tile(gbt_ref[...], reps)
    i_fm = _softcap(g_fm[0:V7X_SUBLANES])
    f_fm = _log_sigmoid(_softcap(g_fm[V7X_SUBLANES:]))
    hi, mid, lo = _split3(f_fm)
    stacked = jnp.concatenate([hi, mid, lo, jnp.zeros_like(hi)], axis=0)
    cs = _dot(stacked.astype(BF16), upper)
    b_fm = cs[0:8] + cs[8:16] + cs[16:24]
    r_fm = i_fm - b_fm
    m_prev = _lane_tile(m_ref[...], reps)
    b_last = jnp.broadcast_to(b_fm[:, L - 1:L], (V7X_SUBLANES, L))
    inter = b_fm + m_prev
    m_new = b_last + jnp.maximum(m_prev, jnp.max(r_fm, axis=1, keepdims=True))
    decay = jnp.exp(b_last + m_prev - m_new)
    w_fm = jnp.exp(b_last + r_fm - m_new)
    m_ref[...] = m_new[:, 0:V7X_LANES]

    k_all = _dot(h, wk_ref[...]).astype(BF16)
    ng = ng_ref[...]

    for hd in range(ML_HEADS):
        q0 = hd * ML_QK_DIM
        v0 = ML_QK_W + hd * ML_V_DIM
        o0 = ML_QK_W + ML_V_W + hd * ML_V_DIM
        qt = (_dot_nt(wfm_ref[q0:q0 + ML_QK_DIM, :], h)
              * (ML_QK_DIM ** -0.5)).astype(BF16)
        vt = _dot_nt(wfm_ref[v0:v0 + ML_V_DIM, :], h).astype(BF16)
        ot = _dot_nt(wfm_ref[o0:o0 + ML_V_DIM, :], h)
        kh = k_all[:, q0:q0 + ML_QK_DIM]

        d_log = jnp.where(key_before_query,
                          r_tm[:, hd:hd + 1] + b_fm[hd:hd + 1, :], -jnp.inf)
        inter_h = inter[hd:hd + 1, :]
        m_t = jnp.maximum(inter_h, jnp.max(d_log, axis=0, keepdims=True))
        p = jnp.exp(d_log - m_t)
        sc = _dot(kh, qt) * p
        a = jnp.exp(inter_h - m_t)
        st_old = state_ref[hd]
        cq = _dot(st_old.astype(BF16), qt)
        num = a * cq[0:ML_V_DIM] + _dot(vt, sc.astype(BF16))
        den = a * cq[ML_V_DIM:ML_V_DIM + 1] + jnp.sum(sc, axis=0, keepdims=True)
        hc = num * (1.0 / jnp.maximum(jnp.abs(den), jnp.exp(-m_t)))

        w_row = w_fm[hd:hd + 1, :]
        n_rows = jnp.where(
            lax.broadcasted_iota(jnp.int32, (2 * V7X_SUBLANES, L), 0) == 0,
            w_row, 0.0)
        vw = jnp.concatenate([vt.astype(F32) * w_row, n_rows],
                             axis=0).astype(BF16)
        state_ref[hd] = decay[hd:hd + 1, 0:ML_QK_DIM] * st_old + _dot(vw, kh)

        hn = hc * lax.rsqrt(jnp.mean(hc * hc, axis=0, keepdims=True) + EPS)
        ng_h = _lane_tile(ng[hd * ML_V_DIM:(hd + 1) * ML_V_DIM, :], reps)
        y_ref[hd * ML_V_DIM:(hd + 1) * ML_V_DIM, :] = (
            hn * ng_h * jax.nn.sigmoid(ot)).astype(BF16)

    out = _dot_tn(y_ref[...], wout_ref[...])
    return x + (1.0 + mod[2:3]) * out


def _mlstm_layer(x, mods, layer, norm_g, w_fm, w_k, w_g, w_gt, gate_b, gate_bt,
                 ng_fm, w_out, j):
    bsz, s, d = x.shape
    L = ML_SEQ_TILE
    tok = lambda b, i: (b, i, 0)
    per_j = lambda b, i: (j, 0, 0)
    return pl.pallas_call(
        _mlstm_kernel,
        grid=(bsz, s // L),
        in_specs=[
            pl.BlockSpec((None, L, d), tok),
            pl.BlockSpec((None, None, 6, d), lambda b, i: (layer, b, 0, 0)),
            pl.BlockSpec((None, 1, d), lambda b, i: (layer, 0, 0)),
            _resident((None, ML_FM_ROWS, d), per_j),
            _resident((None, d, ML_QK_W), per_j),
            _resident((None, d, V7X_LANES), per_j),
            _resident((None, ML_GATE_ROWS, d), per_j),
            pl.BlockSpec((None, 1, V7X_LANES), per_j),
            pl.BlockSpec((None, ML_GATE_ROWS, V7X_LANES), per_j),
            _resident((None, ML_V_W, V7X_LANES), per_j),
            _resident((None, ML_V_W, d), per_j),
        ],
        out_specs=pl.BlockSpec((None, L, d), tok),
        out_shape=jax.ShapeDtypeStruct(x.shape, F32),
        scratch_shapes=[
            pltpu.VMEM((ML_HEADS, ML_STATE_ROWS, ML_QK_DIM), F32),
            pltpu.VMEM((V7X_SUBLANES, V7X_LANES), F32),
            pltpu.VMEM((L // ML_CHUNK, ML_V_W, ML_CHUNK), BF16),
        ],
        compiler_params=_compiler_params(),
        name=f"mlstm_layer{layer}",
    )(x, mods, norm_g, w_fm, w_k, w_g, w_gt, gate_b, gate_bt, ng_fm, w_out)


def _mlstm_weights(ml_w_in, ml_b_i, ml_b_f, ml_norm_g):
    n_ml = ml_w_in.shape[0]
    q, k, v, o, gi, gf = jnp.split(
        ml_w_in, [ML_QK_W, 2 * ML_QK_W, 2 * ML_QK_W + ML_V_W,
                  2 * ML_QK_W + 2 * ML_V_W, 2 * ML_QK_W + 2 * ML_V_W + ML_HEADS],
        axis=-1)
    w_fm = jnp.concatenate([q, v, o], axis=-1).transpose(0, 2, 1).astype(BF16)
    w_k = k.astype(BF16)
    lane_pad = V7X_LANES - 2 * ML_HEADS
    w_g = jnp.pad(jnp.concatenate([gi, gf], axis=-1),
                  ((0, 0), (0, 0), (0, lane_pad))).astype(BF16)
    row_pad = V7X_SUBLANES - ML_HEADS
    pad_rows = lambda t: jnp.pad(t, ((0, 0), (0, row_pad), (0, 0)))
    w_gt = jnp.concatenate([pad_rows(gi.transpose(0, 2, 1)),
                            pad_rows(gf.transpose(0, 2, 1))], axis=1).astype(BF16)
    gate_b = jnp.pad(jnp.concatenate([ml_b_i, ml_b_f], axis=-1),
                     ((0, 0), (0, lane_pad))).reshape(n_ml, 1, V7X_LANES)
    bt = jnp.concatenate([pad_rows(ml_b_i[:, :, None]),
                          pad_rows(ml_b_f[:, :, None])], axis=1)
    gate_bt = jnp.broadcast_to(bt, (n_ml, ML_GATE_ROWS, V7X_LANES))
    ng_fm = jnp.broadcast_to(ml_norm_g[:, :, None], (n_ml, ML_V_W, V7X_LANES))
    return w_fm, w_k, w_g, w_gt, gate_b, gate_bt, ng_fm


def kernel(x, c, ada_w, ada_b, norm_mix_g, norm_ffn_g, sc_w_in, sc_conv_w, sc_w_out, ml_w_in, ml_b_i, ml_b_f, ml_norm_g, ml_w_out, ffn_w_up, ffn_conv_w, ffn_conv_b, ffn_w_down, final_norm_g):
    mods = _ada_modulation(c, ada_w, ada_b)

    sc_w_in_b = sc_w_in.astype(BF16)
    sc_w_out_b = sc_w_out.astype(BF16)
    ml_weights = _mlstm_weights(ml_w_in, ml_b_i, ml_b_f, ml_norm_g)
    ml_w_out_b = ml_w_out.astype(BF16)
    ffn_w_up_b = ffn_w_up.astype(BF16)
    ffn_w_down_b = ffn_w_down.astype(BF16)
    norm_mix_g = norm_mix_g.reshape(DEPTH, 1, D_MODEL)
    norm_ffn_g = norm_ffn_g.reshape(DEPTH, 1, D_MODEL)

    for layer in range(DEPTH):
        j = layer // 2
        if layer % 2 == 0:
            x = _sconv_layer(x, mods, layer, norm_mix_g, sc_w_in_b, sc_conv_w,
                             sc_w_out_b, j)
        else:
            x = _mlstm_layer(x, mods, layer, norm_mix_g, *ml_weights,
                             ml_w_out_b, j)
        x = _ffn_layer(x, mods, layer, norm_ffn_g, ffn_w_up_b, ffn_conv_w,
                       ffn_conv_b, ffn_w_down_b, final_norm_g,
                       final_norm=(layer == DEPTH - 1))
    return x
```

```python
import functools

import jax
import jax.numpy as jnp
from jax import lax
from jax.experimental import pallas as pl
from jax.experimental.pallas import tpu as pltpu

D_MODEL = 1024
DEPTH = 4
CONV_WIDTH = 3
ML_HEADS = 4
ML_V_DIM = D_MODEL // ML_HEADS
ML_QK_DIM = ML_V_DIM // 2
ML_QK_W = ML_HEADS * ML_QK_DIM
ML_V_W = ML_HEADS * ML_V_DIM
GATE_SOFTCAP = 15.0
D_FF = 2816
EPS = 1e-6

V7X_LANES = 128
V7X_SUBLANES = 8
V7X_MXU_DIM = 256
V7X_VMEM_BYTES = 64 * 1024 * 1024
VMEM_LIMIT_BYTES = V7X_VMEM_BYTES - 8 * 1024 * 1024

ML_CHUNK = 256
ML_SEQ_TILE = 512
FFN_SEQ_TILE = 512
SCONV_SEQ_TILE = 512
SCONV_STREAM_ROWS = 256
FFN_STREAM_ROWS = 512
FF_CHUNK = V7X_MXU_DIM
SC_CHUNK = V7X_MXU_DIM
ADA_TILE = 1536

BF16 = jnp.bfloat16
F32 = jnp.float32


def _dot(a, b):
    return jnp.dot(a, b, preferred_element_type=F32)


def _norm_modulate(x, g, shift, scale):
    ms = jnp.mean(x * x, axis=-1, keepdims=True)
    y = x * lax.rsqrt(ms + EPS)
    return (y * g) * (1.0 + scale) + shift


def _carry_halo(hist_ref, t):
    halo = V7X_SUBLANES
    hist_ref[0:halo, :] = hist_ref[t:t + halo, :]


def _causal_conv3(p, hist_ref, r0, cols, w):
    n = p.shape[0]
    base = V7X_SUBLANES + r0
    hist_ref[base:base + n, cols] = p
    p1 = hist_ref[base - 1:base - 1 + n, cols]
    p2 = hist_ref[base - 2:base - 2 + n, cols]
    return w[0:1] * p2 + w[1:2] * p1 + w[2:3] * p


def _resident(block_shape, index_map):
    return pl.BlockSpec(block_shape, index_map, pipeline_mode=pl.Buffered(1))


def _compiler_params():
    return pltpu.CompilerParams(
        dimension_semantics=("arbitrary", "arbitrary"),
        vmem_limit_bytes=VMEM_LIMIT_BYTES)


def _ada_kernel(c_ref, w_ref, b_ref, o_ref):
    c = c_ref[...]
    cond = (c * jax.nn.sigmoid(c)).astype(BF16)
    o_ref[...] = _dot(cond, w_ref[...].astype(BF16)) + b_ref[...]


def _ada_modulation(c, ada_w, ada_b):
    bsz, d = c.shape
    n = ada_w.shape[-1]
    out = pl.pallas_call(
        _ada_kernel,
        grid=(DEPTH, n // ADA_TILE),
        in_specs=[
            pl.BlockSpec((bsz, d), lambda l, j: (0, 0)),
            pl.BlockSpec((None, d, ADA_TILE), lambda l, j: (l, 0, j)),
            pl.BlockSpec((None, 1, ADA_TILE), lambda l, j: (l, 0, j)),
        ],
        out_specs=pl.BlockSpec((None, bsz, ADA_TILE), lambda l, j: (l, 0, j)),
        out_shape=jax.ShapeDtypeStruct((DEPTH, bsz, n), F32),
        compiler_params=_compiler_params(),
        name="adaln_mod",
    )(c, ada_w, ada_b.reshape(DEPTH, 1, n))
    return out.reshape(DEPTH, bsz, 6, d)


def _sconv_kernel(x_ref, mod_ref, g_ref, win_ref, cw_ref, wout_ref, o_ref,
                  hist_ref, y_ref):
    t = x_ref.shape[0]

    @pl.when(pl.program_id(1) == 0)
    def _():
        hist_ref[t:, :] = jnp.zeros((V7X_SUBLANES, D_MODEL), F32)

    _carry_halo(hist_ref, t)
    mod = mod_ref[...]
    for r0 in range(0, t, SCONV_STREAM_ROWS):
        rows = slice(r0, r0 + SCONV_STREAM_ROWS)
        xs = x_ref[rows, :]
        h = _norm_modulate(xs, g_ref[...], mod[0:1], mod[1:2]).astype(BF16)
        for j in range(D_MODEL // SC_CHUNK):
            lo = j * SC_CHUNK
            gb = _dot(h, win_ref[:, lo:lo + SC_CHUNK])
            gc = _dot(h, win_ref[:, D_MODEL + lo:D_MODEL + lo + SC_CHUNK])
            u = _dot(h, win_ref[:, 2 * D_MODEL + lo:2 * D_MODEL + lo + SC_CHUNK])
            conv = _causal_conv3(gc * u, hist_ref, r0, slice(lo, lo + SC_CHUNK),
                                 cw_ref[:, lo:lo + SC_CHUNK])
            y_ref[rows, lo:lo + SC_CHUNK] = (gb * conv).astype(BF16)
        out = _dot(y_ref[rows, :], wout_ref[...])
        o_ref[rows, :] = xs + (1.0 + mod[2:3]) * out


def _sconv_layer(x, mods, layer, norm_g, w_in, conv_w, w_out, j):
    bsz, s, d = x.shape
    tm = SCONV_SEQ_TILE
    return pl.pallas_call(
        _sconv_kernel,
        grid=(bsz, s // tm),
        in_specs=[
            pl.BlockSpec((None, tm, d), lambda b, i: (b, i, 0)),
            pl.BlockSpec((None, None, 6, d), lambda b, i: (layer, b, 0, 0)),
            pl.BlockSpec((None, 1, d), lambda b, i: (layer, 0, 0)),
            _resident((None, d, 3 * d), lambda b, i: (j, 0, 0)),
            pl.BlockSpec((None, CONV_WIDTH, d), lambda b, i: (j, 0, 0)),
            _resident((None, d, d), lambda b, i: (j, 0, 0)),
        ],
        out_specs=pl.BlockSpec((None, tm, d), lambda b, i: (b, i, 0)),
        out_shape=jax.ShapeDtypeStruct(x.shape, F32),
        scratch_shapes=[
            pltpu.VMEM((tm + V7X_SUBLANES, d), F32),
            pltpu.VMEM((tm, d), BF16),
        ],
        compiler_params=_compiler_params(),
        name=f"sconv_layer{layer}",
    )(x, mods, norm_g, w_in, conv_w, w_out)


def _ffn_kernel(x_ref, mod_ref, g_ref, wup_ref, cw_ref, cb_ref, wdown_ref,
                fg_ref, o_ref, hist_ref, a_ref, *, final_norm):
    t = x_ref.shape[0]

    @pl.when(pl.program_id(1) == 0)
    def _():
        hist_ref[t:, :] = jnp.zeros((V7X_SUBLANES, 2 * D_FF), F32)

    _carry_halo(hist_ref, t)
    mod = mod_ref[...]
    for r0 in range(0, t, FFN_STREAM_ROWS):
        rows = slice(r0, r0 + FFN_STREAM_ROWS)
        xs = x_ref[rows, :]
        h = _norm_modulate(xs, g_ref[...], mod[3:4], mod[4:5]).astype(BF16)
        for j in range(D_FF // FF_CHUNK):
            halves = []
            for base in (0, D_FF):
                lo = base + j * FF_CHUNK
                cols = slice(lo, lo + FF_CHUNK)
                p = _dot(h, wup_ref[:, cols])
                halves.append(
                    _causal_conv3(p, hist_ref, r0, cols, cw_ref[:, cols])
                    + cb_ref[:, cols])
            gate, val = halves
            act = gate * jax.nn.sigmoid(gate) * val
            a_ref[rows, j * FF_CHUNK:(j + 1) * FF_CHUNK] = act.astype(BF16)
        out = _dot(a_ref[rows, :], wdown_ref[...])
        y = xs + (1.0 + mod[5:6]) * out
        if final_norm:
            ms = jnp.mean(y * y, axis=-1, keepdims=True)
            y = y * lax.rsqrt(ms + EPS) * fg_ref[...]
        o_ref[rows, :] = y


def _ffn_layer(x, mods, layer, norm_g, w_up, conv_w, conv_b, w_down, final_g,
               final_norm):
    bsz, s, d = x.shape
    tm = FFN_SEQ_TILE
    f2 = 2 * D_FF
    return pl.pallas_call(
        functools.partial(_ffn_kernel, final_norm=final_norm),
        grid=(bsz, s // tm),
        in_specs=[
            pl.BlockSpec((None, tm, d), lambda b, i: (b, i, 0)),
            pl.BlockSpec((None, None, 6, d), lambda b, i: (layer, b, 0, 0)),
            pl.BlockSpec((None, 1, d), lambda b, i: (layer, 0, 0)),
            _resident((None, d, f2), lambda b, i: (layer, 0, 0)),
            pl.BlockSpec((None, CONV_WIDTH, f2), lambda b, i: (layer, 0, 0)),
            pl.BlockSpec((None, 1, f2), lambda b, i: (layer, 0, 0)),
            _resident((None, D_FF, d), lambda b, i: (layer, 0, 0)),
            pl.BlockSpec((1, d), lambda b, i: (0, 0)),
        ],
        out_specs=pl.BlockSpec((None, tm, d), lambda b, i: (b, i, 0)),
        out_shape=jax.ShapeDtypeStruct(x.shape, F32),
        scratch_shapes=[
            pltpu.VMEM((tm + V7X_SUBLANES, f2), F32),
            pltpu.VMEM((tm, D_FF), BF16),
        ],
        compiler_params=_compiler_params(),
        name=f"ffn_layer{layer}",
    )(x, mods, norm_g, w_up, conv_w, conv_b.reshape(DEPTH, 1, f2), w_down,
      final_g.reshape(1, d))


ML_STATE_ROWS = ML_V_DIM + 16
ML_FM_ROWS = ML_QK_W + 2 * ML_V_W
ML_GATE_ROWS = 2 * V7X_SUBLANES


def _log_sigmoid(z):
    return jnp.minimum(z, 0.0) - jnp.log1p(jnp.exp(-jnp.abs(z)))


def _softcap(z):
    return GATE_SOFTCAP * jnp.tanh(z * (1.0 / GATE_SOFTCAP))


def _split3(z):
    hi = z.astype(BF16).astype(F32)
    r1 = z - hi
    mid = r1.astype(BF16).astype(F32)
    lo = (r1 - mid).astype(BF16).astype(F32)
    return hi, mid, lo


def _lane_tile(a, reps):
    return jnp.concatenate([a] * reps, axis=1)


def _dot_nt(a, b):
    return lax.dot_general(a, b, (((1,), (1,)), ((), ())),
                           preferred_element_type=F32)


def _dot_tn(a, b):
    return lax.dot_general(a, b, (((0,), (0,)), ((), ())),
                           preferred_element_type=F32)


def _mlstm_kernel(x_ref, mod_ref, g_ref, wfm_ref, wk_ref, wgt_ref,
                  gbt_ref, ng_ref, wout_ref, out_ref,
                  state_ref, m_ref, y_ref):
    @pl.when(pl.program_id(1) == 0)
    def _():
        state_ref[...] = jnp.zeros_like(state_ref)
        m_ref[...] = jnp.zeros_like(m_ref)

    for c in range(x_ref.shape[0] // ML_CHUNK):
        rows = slice(c * ML_CHUNK, (c + 1) * ML_CHUNK)
        out_ref[rows, :] = _mlstm_chunk(
            x_ref[rows, :], mod_ref[...], g_ref, wfm_ref, wk_ref, wgt_ref,
            gbt_ref, ng_ref, wout_ref, state_ref, m_ref, y_ref.at[c])


def _mlstm_chunk(x, mod, g_ref, wfm_ref, wk_ref, wgt_ref, gbt_ref, ng_ref,
                 wout_ref, state_ref, m_ref, y_ref):
    L = x.shape[0]
    reps = L // V7X_LANES
    h = _norm_modulate(x, g_ref[...], mod[0:1], mod[1:2]).astype(BF16)

    row = lax.broadcasted_iota(jnp.int32, (L, L), 0)
    col = lax.broadcasted_iota(jnp.int32, (L, L), 1)
    upper = jnp.where(row <= col, 1.0, 0.0).astype(BF16)
    key_before_query = row <= col

    g_fm = _dot_nt(wgt_ref[...], h) + _lane_tile(gbt_ref[...], reps)
    i_fm = _softcap(g_fm[0:V7X_SUBLANES])
    f_fm = _log_sigmoid(_softcap(g_fm[V7X_SUBLANES:]))
    hi, mid, lo = _split3(f_fm)
    stacked = jnp.concatenate([hi, mid, lo, jnp.zeros_like(hi)], axis=0)
    cs = _dot(stacked.astype(BF16), upper)
    b_fm = cs[0:8] + cs[8:16] + cs[16:24]
    r_fm = i_fm - b_fm
    r_tm = jnp.concatenate(
        [r_fm, jnp.zeros((V7X_LANES - V7X_SUBLANES, L), F32)], axis=0).T
    m_prev = _lane_tile(m_ref[...], reps)
    b_last = jnp.broadcast_to(b_fm[:, L - 1:L], (V7X_SUBLANES, L))
    inter = b_fm + m_prev
    m_new = b_last + jnp.maximum(m_prev, jnp.max(r_fm, axis=1, keepdims=True))
    decay = jnp.exp(b_last + m_prev - m_new)
    w_fm = jnp.exp(b_last + r_fm - m_new)
    m_ref[...] = m_new[:, 0:V7X_LANES]

    k_all = _dot(h, wk_ref[...]).astype(BF16)
    ng = ng_ref[...]

    for hd in range(ML_HEADS):
        q0 = hd * ML_QK_DIM
        v0 = ML_QK_W + hd * ML_V_DIM
        o0 = ML_QK_W + ML_V_W + hd * ML_V_DIM
        qt = (_dot_nt(wfm_ref[q0:q0 + ML_QK_DIM, :], h)
              * (ML_QK_DIM ** -0.5)).astype(BF16)
        vt = _dot_nt(wfm_ref[v0:v0 + ML_V_DIM, :], h).astype(BF16)
        ot = _dot_nt(wfm_ref[o0:o0 + ML_V_DIM, :], h)
        kh = k_all[:, q0:q0 + ML_QK_DIM]

        d_log = jnp.where(key_before_query,
                          r_tm[:, hd:hd + 1] + b_fm[hd:hd + 1, :], -jnp.inf)
        inter_h = inter[hd:hd + 1, :]
        m_t = jnp.maximum(inter_h, jnp.max(d_log, axis=0, keepdims=True))
        p = jnp.exp(d_log - m_t)
        sc = _dot(kh, qt) * p
        a = jnp.exp(inter_h - m_t)
        st_old = state_ref[hd]
        cq = _dot(st_old.astype(BF16), qt)
        num = a * cq[0:ML_V_DIM] + _dot(vt, sc.astype(BF16))
        den = a * cq[ML_V_DIM:ML_V_DIM + 1] + jnp.sum(sc, axis=0, keepdims=True)
        hc = num * (1.0 / jnp.maximum(jnp.abs(den), jnp.exp(-m_t)))

        w_row = w_fm[hd:hd + 1, :]
        n_rows = jnp.where(
            lax.broadcasted_iota(jnp.int32, (2 * V7X_SUBLANES, L), 0) == 0,
            w_row, 0.0)
        vw = jnp.concatenate([vt.astype(F32) * w_row, n_rows],
                             axis=0).astype(BF16)
        state_ref[hd] = decay[hd:hd + 1, 0:ML_QK_DIM] * st_old + _dot(vw, kh)

        hn = hc * lax.rsqrt(jnp.mean(hc * hc, axis=0, keepdims=True) + EPS)
        ng_h = _lane_tile(ng[hd * ML_V_DIM:(hd + 1) * ML_V_DIM, :], reps)
        y_ref[hd * ML_V_DIM:(hd + 1) * ML_V_DIM, :] = (
            hn * ng_h * jax.nn.sigmoid(ot)).astype(BF16)

    out = _dot_tn(y_ref[...], wout_ref[...])
    return x + (1.0 + mod[2:3]) * out


def _mlstm_layer(x, mods, layer, norm_g, w_fm, w_k, w_gt, gate_bt, ng_fm,
                 w_out, j):
    bsz, s, d = x.shape
    L = ML_SEQ_TILE
    tok = lambda b, i: (b, i, 0)
    per_j = lambda b, i: (j, 0, 0)
    return pl.pallas_call(
        _mlstm_kernel,
        grid=(bsz, s // L),
        in_specs=[
            pl.BlockSpec((None, L, d), tok),
            pl.BlockSpec((None, None, 6, d), lambda b, i: (layer, b, 0, 0)),
            pl.BlockSpec((None, 1, d), lambda b, i: (layer, 0, 0)),
            _resident((None, ML_FM_ROWS, d), per_j),
            _resident((None, d, ML_QK_W), per_j),
            _resident((None, ML_GATE_ROWS, d), per_j),
            pl.BlockSpec((None, ML_GATE_ROWS, V7X_LANES), per_j),
            _resident((None, ML_V_W, V7X_LANES), per_j),
            _resident((None, ML_V_W, d), per_j),
        ],
        out_specs=pl.BlockSpec((None, L, d), tok),
        out_shape=jax.ShapeDtypeStruct(x.shape, F32),
        scratch_shapes=[
            pltpu.VMEM((ML_HEADS, ML_STATE_ROWS, ML_QK_DIM), F32),
            pltpu.VMEM((V7X_SUBLANES, V7X_LANES), F32),
            pltpu.VMEM((L // ML_CHUNK, ML_V_W, ML_CHUNK), BF16),
        ],
        compiler_params=_compiler_params(),
        name=f"mlstm_layer{layer}",
    )(x, mods, norm_g, w_fm, w_k, w_gt, gate_bt, ng_fm, w_out)


def _mlstm_weights(ml_w_in, ml_b_i, ml_b_f, ml_norm_g):
    n_ml = ml_w_in.shape[0]
    q, k, v, o, gi, gf = jnp.split(
        ml_w_in, [ML_QK_W, 2 * ML_QK_W, 2 * ML_QK_W + ML_V_W,
                  2 * ML_QK_W + 2 * ML_V_W, 2 * ML_QK_W + 2 * ML_V_W + ML_HEADS],
        axis=-1)
    w_fm = jnp.concatenate([q, v, o], axis=-1).transpose(0, 2, 1).astype(BF16)
    w_k = k.astype(BF16)
    row_pad = V7X_SUBLANES - ML_HEADS
    pad_rows = lambda t: jnp.pad(t, ((0, 0), (0, row_pad), (0, 0)))
    w_gt = jnp.concatenate([pad_rows(gi.transpose(0, 2, 1)),
                            pad_rows(gf.transpose(0, 2, 1))], axis=1).astype(BF16)
    bt = jnp.concatenate([pad_rows(ml_b_i[:, :, None]),
                          pad_rows(ml_b_f[:, :, None])], axis=1)
    gate_bt = jnp.broadcast_to(bt, (n_ml, ML_GATE_ROWS, V7X_LANES))
    ng_fm = jnp.broadcast_to(ml_norm_g[:, :, None], (n_ml, ML_V_W, V7X_LANES))
    return w_fm, w_k, w_gt, gate_bt, ng_fm


def kernel(x, c, ada_w, ada_b, norm_mix_g, norm_ffn_g, sc_w_in, sc_conv_w, sc_w_out, ml_w_in, ml_b_i, ml_b_f, ml_norm_g, ml_w_out, ffn_w_up, ffn_conv_w, ffn_conv_b, ffn_w_down, final_norm_g):
    mods = _ada_modulation(c, ada_w, ada_b)

    sc_w_in_b = sc_w_in.astype(BF16)
    sc_w_out_b = sc_w_out.astype(BF16)
    ml_weights = _mlstm_weights(ml_w_in, ml_b_i, ml_b_f, ml_norm_g)
    ml_w_out_b = ml_w_out.astype(BF16)
    ffn_w_up_b = ffn_w_up.astype(BF16)
    ffn_w_down_b = ffn_w_down.astype(BF16)
    norm_mix_g = norm_mix_g.reshape(DEPTH, 1, D_MODEL)
    norm_ffn_g = norm_ffn_g.reshape(DEPTH, 1, D_MODEL)

    for layer in range(DEPTH):
        j = layer // 2
        if layer % 2 == 0:
            x = _sconv_layer(x, mods, layer, norm_mix_g, sc_w_in_b, sc_conv_w,
                             sc_w_out_b, j)
        else:
            x = _mlstm_layer(x, mods, layer, norm_mix_g, *ml_weights,
                             ml_w_out_b, j)
        x = _ffn_layer(x, mods, layer, norm_ffn_g, ffn_w_up_b, ffn_conv_w,
                       ffn_conv_b, ffn_w_down_b, final_norm_g,
                       final_norm=(layer == DEPTH - 1))
    return x
```

```python
import functools

import jax
import jax.numpy as jnp
from jax import lax
from jax.experimental import pallas as pl
from jax.experimental.pallas import tpu as pltpu

D_MODEL = 1024
DEPTH = 4
CONV_WIDTH = 3
ML_HEADS = 4
ML_V_DIM = D_MODEL // ML_HEADS
ML_QK_DIM = ML_V_DIM // 2
ML_QK_W = ML_HEADS * ML_QK_DIM
ML_V_W = ML_HEADS * ML_V_DIM
GATE_SOFTCAP = 15.0
D_FF = 2816
EPS = 1e-6

V7X_LANES = 128
V7X_SUBLANES = 8
V7X_MXU_DIM = 256
V7X_VMEM_BYTES = 64 * 1024 * 1024
VMEM_LIMIT_BYTES = V7X_VMEM_BYTES - 8 * 1024 * 1024

ML_CHUNK = 256
ML_SEQ_TILE = 512
FFN_SEQ_TILE = 512
SCONV_SEQ_TILE = 512
SCONV_STREAM_ROWS = 256
FFN_STREAM_ROWS = 512
FF_CHUNK = V7X_MXU_DIM
SC_CHUNK = V7X_MXU_DIM
ADA_TILE = 1536

BF16 = jnp.bfloat16
F32 = jnp.float32


def _dot(a, b):
    return jnp.dot(a, b, preferred_element_type=F32)


def _norm_modulate(x, g, shift, scale):
    ms = jnp.mean(x * x, axis=-1, keepdims=True)
    y = x * lax.rsqrt(ms + EPS)
    return (y * g) * (1.0 + scale) + shift


def _carry_halo(hist_ref, t):
    halo = V7X_SUBLANES
    hist_ref[0:halo, :] = hist_ref[t:t + halo, :]


def _causal_conv3(p, hist_ref, r0, cols, w):
    n = p.shape[0]
    base = V7X_SUBLANES + r0
    hist_ref[base:base + n, cols] = p
    p1 = hist_ref[base - 1:base - 1 + n, cols]
    p2 = hist_ref[base - 2:base - 2 + n, cols]
    return w[0:1] * p2 + w[1:2] * p1 + w[2:3] * p


def _resident(block_shape, index_map):
    return pl.BlockSpec(block_shape, index_map, pipeline_mode=pl.Buffered(1))


def _compiler_params():
    return pltpu.CompilerParams(
        dimension_semantics=("arbitrary", "arbitrary"),
        vmem_limit_bytes=VMEM_LIMIT_BYTES)


def _ada_kernel(c_ref, w_ref, b_ref, o_ref):
    c = c_ref[...]
    cond = (c * jax.nn.sigmoid(c)).astype(BF16)
    o_ref[...] = _dot(cond, w_ref[...].astype(BF16)) + b_ref[...]


def _ada_modulation(c, ada_w, ada_b):
    bsz, d = c.shape
    n = ada_w.shape[-1]
    out = pl.pallas_call(
        _ada_kernel,
        grid=(DEPTH, n // ADA_TILE),
        in_specs=[
            pl.BlockSpec((bsz, d), lambda l, j: (0, 0)),
            pl.BlockSpec((None, d, ADA_TILE), lambda l, j: (l, 0, j)),
            pl.BlockSpec((None, 1, ADA_TILE), lambda l, j: (l, 0, j)),
        ],
        out_specs=pl.BlockSpec((None, bsz, ADA_TILE), lambda l, j: (l, 0, j)),
        out_shape=jax.ShapeDtypeStruct((DEPTH, bsz, n), F32),
        compiler_params=_compiler_params(),
        name="adaln_mod",
    )(c, ada_w, ada_b.reshape(DEPTH, 1, n))
    return out.reshape(DEPTH, bsz, 6, d)


def _sconv_kernel(x_ref, mod_ref, g_ref, win_ref, cw_ref, wout_ref, o_ref,
                  hist_ref, y_ref):
    t = x_ref.shape[0]

    @pl.when(pl.program_id(1) == 0)
    def _():
        hist_ref[t:, :] = jnp.zeros((V7X_SUBLANES, D_MODEL), F32)

    _carry_halo(hist_ref, t)
    mod = mod_ref[...]
    for r0 in range(0, t, SCONV_STREAM_ROWS):
        rows = slice(r0, r0 + SCONV_STREAM_ROWS)
        xs = x_ref[rows, :]
        h = _norm_modulate(xs, g_ref[...], mod[0:1], mod[1:2]).astype(BF16)
        for j in range(D_MODEL // SC_CHUNK):
            lo = j * SC_CHUNK
            gb = _dot(h, win_ref[:, lo:lo + SC_CHUNK])
            gc = _dot(h, win_ref[:, D_MODEL + lo:D_MODEL + lo + SC_CHUNK])
            u = _dot(h, win_ref[:, 2 * D_MODEL + lo:2 * D_MODEL + lo + SC_CHUNK])
            conv = _causal_conv3(gc * u, hist_ref, r0, slice(lo, lo + SC_CHUNK),
                                 cw_ref[:, lo:lo + SC_CHUNK])
            y_ref[rows, lo:lo + SC_CHUNK] = (gb * conv).astype(BF16)
        out = _dot(y_ref[rows, :], wout_ref[...])
        o_ref[rows, :] = xs + (1.0 + mod[2:3]) * out


def _sconv_layer(x, mods, layer, norm_g, w_in, conv_w, w_out, j):
    bsz, s, d = x.shape
    tm = SCONV_SEQ_TILE
    return pl.pallas_call(
        _sconv_kernel,
        grid=(bsz, s // tm),
        in_specs=[
            pl.BlockSpec((None, tm, d), lambda b, i: (b, i, 0)),
            pl.BlockSpec((None, None, 6, d), lambda b, i: (layer, b, 0, 0)),
            pl.BlockSpec((None, 1, d), lambda b, i: (layer, 0, 0)),
            _resident((None, d, 3 * d), lambda b, i: (j, 0, 0)),
            pl.BlockSpec((None, CONV_WIDTH, d), lambda b, i: (j, 0, 0)),
            _resident((None, d, d), lambda b, i: (j, 0, 0)),
        ],
        out_specs=pl.BlockSpec((None, tm, d), lambda b, i: (b, i, 0)),
        out_shape=jax.ShapeDtypeStruct(x.shape, F32),
        scratch_shapes=[
            pltpu.VMEM((tm + V7X_SUBLANES, d), F32),
            pltpu.VMEM((tm, d), BF16),
        ],
        compiler_params=_compiler_params(),
        name=f"sconv_layer{layer}",
    )(x, mods, norm_g, w_in, conv_w, w_out)


def _ffn_kernel(x_ref, mod_ref, g_ref, wup_ref, cw_ref, cb_ref, wdown_ref,
                fg_ref, o_ref, hist_ref, a_ref, *, final_norm):
    t = x_ref.shape[0]

    @pl.when(pl.program_id(1) == 0)
    def _():
        hist_ref[t:, :] = jnp.zeros((V7X_SUBLANES, 2 * D_FF), F32)

    _carry_halo(hist_ref, t)
    mod = mod_ref[...]
    for r0 in range(0, t, FFN_STREAM_ROWS):
        rows = slice(r0, r0 + FFN_STREAM_ROWS)
        xs = x_ref[rows, :]
        h = _norm_modulate(xs, g_ref[...], mod[3:4], mod[4:5]).astype(BF16)
        for j in range(D_FF // FF_CHUNK):
            halves = []
            for base in (0, D_FF):
                lo = base + j * FF_CHUNK
                cols = slice(lo, lo + FF_CHUNK)
                p = _dot(h, wup_ref[:, cols])
                halves.append(
                    _causal_conv3(p, hist_ref, r0, cols, cw_ref[:, cols])
                    + cb_ref[:, cols])
            gate, val = halves
            act = gate * jax.nn.sigmoid(gate) * val
            a_ref[rows, j * FF_CHUNK:(j + 1) * FF_CHUNK] = act.astype(BF16)
        out = _dot(a_ref[rows, :], wdown_ref[...])
        y = xs + (1.0 + mod[5:6]) * out
        if final_norm:
            ms = jnp.mean(y * y, axis=-1, keepdims=True)
            y = y * lax.rsqrt(ms + EPS) * fg_ref[...]
        o_ref[rows, :] = y


def _ffn_layer(x, mods, layer, norm_g, w_up, conv_w, conv_b, w_down, final_g,
               final_norm):
    bsz, s, d = x.shape
    tm = FFN_SEQ_TILE
    f2 = 2 * D_FF
    return pl.pallas_call(
        functools.partial(_ffn_kernel, final_norm=final_norm),
        grid=(bsz, s // tm),
        in_specs=[
            pl.BlockSpec((None, tm, d), lambda b, i: (b, i, 0)),
            pl.BlockSpec((None, None, 6, d), lambda b, i: (layer, b, 0, 0)),
            pl.BlockSpec((None, 1, d), lambda b, i: (layer, 0, 0)),
            _resident((None, d, f2), lambda b, i: (layer, 0, 0)),
            pl.BlockSpec((None, CONV_WIDTH, f2), lambda b, i: (layer, 0, 0)),
            pl.BlockSpec((None, 1, f2), lambda b, i: (layer, 0, 0)),
            _resident((None, D_FF, d), lambda b, i: (layer, 0, 0)),
            pl.BlockSpec((1, d), lambda b, i: (0, 0)),
        ],
        out_specs=pl.BlockSpec((None, tm, d), lambda b, i: (b, i, 0)),
        out_shape=jax.ShapeDtypeStruct(x.shape, F32),
        scratch_shapes=[
            pltpu.VMEM((tm + V7X_SUBLANES, f2), F32),
            pltpu.VMEM((tm, D_FF), BF16),
        ],
        compiler_params=_compiler_params(),
        name=f"ffn_layer{layer}",
    )(x, mods, norm_g, w_up, conv_w, conv_b.reshape(DEPTH, 1, f2), w_down,
      final_g.reshape(1, d))


ML_STATE_ROWS = ML_V_DIM + 16
ML_FM_ROWS = ML_QK_W + 2 * ML_V_W
ML_GATE_ROWS = 2 * V7X_SUBLANES


def _log_sigmoid(z):
    return jnp.minimum(z, 0.0) - jnp.log1p(jnp.exp(-jnp.abs(z)))


def _softcap(z):
    return GATE_SOFTCAP * jnp.tanh(z * (1.0 / GATE_SOFTCAP))


def _split3(z):
    hi = z.astype(BF16).astype(F32)
    r1 = z - hi
    mid = r1.astype(BF16).astype(F32)
    lo = (r1 - mid).astype(BF16).astype(F32)
    return hi, mid, lo


def _lane_tile(a, reps):
    return jnp.concatenate([a] * reps, axis=1)


def _dot_nt(a, b):
    return lax.dot_general(a, b, (((1,), (1,)), ((), ())),
                           preferred_element_type=F32)


def _dot_tn(a, b):
    return lax.dot_general(a, b, (((0,), (0,)), ((), ())),
                           preferred_element_type=F32)


def _mlstm_kernel(x_ref, mod_ref, g_ref, wfm_ref, wk_ref, wgt_ref,
                  gbt_ref, ng_ref, wout_ref, out_ref,
                  state_ref, m_ref, y_ref):
    @pl.when(pl.program_id(1) == 0)
    def _():
        state_ref[...] = jnp.zeros_like(state_ref)
        m_ref[...] = jnp.zeros_like(m_ref)

    for c in range(x_ref.shape[0] // ML_CHUNK):
        rows = slice(c * ML_CHUNK, (c + 1) * ML_CHUNK)
        out_ref[rows, :] = _mlstm_chunk(
            x_ref[rows, :], mod_ref[...], g_ref, wfm_ref, wk_ref, wgt_ref,
            gbt_ref, ng_ref, wout_ref, state_ref, m_ref, y_ref.at[c])


def _mlstm_chunk(x, mod, g_ref, wfm_ref, wk_ref, wgt_ref, gbt_ref, ng_ref,
                 wout_ref, state_ref, m_ref, y_ref):
    L = x.shape[0]
    reps = L // V7X_LANES
    h = _norm_modulate(x, g_ref[...], mod[0:1], mod[1:2]).astype(BF16)

    row = lax.broadcasted_iota(jnp.int32, (L, L), 0)
    col = lax.broadcasted_iota(jnp.int32, (L, L), 1)
    upper = jnp.where(row <= col, 1.0, 0.0).astype(BF16)
    key_before_query = row <= col

    g_fm = _dot_nt(wgt_ref[...], h) + _lane_tile(gbt_ref[...], reps)
    i_fm = _softcap(g_fm[0:V7X_SUBLANES])
    f_fm = _log_sigmoid(_softcap(g_fm[V7X_SUBLANES:]))
    hi, mid, lo = _split3(f_fm)
    stacked = jnp.concatenate([hi, mid, lo, jnp.zeros_like(hi)], axis=0)
    cs = _dot(stacked.astype(BF16), upper)
    b_fm = cs[0:8] + cs[8:16] + cs[16:24]
    r_fm = i_fm - b_fm
    r_tm = jnp.concatenate(
        [r_fm, jnp.zeros((V7X_LANES - V7X_SUBLANES, L), F32)], axis=0).T
    m_prev = _lane_tile(m_ref[...], reps)
    b_last = jnp.broadcast_to(b_fm[:, L - 1:L], (V7X_SUBLANES, L))
    inter = b_fm + m_prev
    m_new = b_last + jnp.maximum(m_prev, jnp.max(r_fm, axis=1, keepdims=True))
    decay = jnp.exp(b_last + m_prev - m_new)
    w_fm = jnp.exp(b_last + r_fm - m_new)
    m_ref[...] = m_new[:, 0:V7X_LANES]

    k_all = _dot(h, wk_ref[...]).astype(BF16)
    qt_all = (_dot_nt(wfm_ref[0:ML_QK_W, :], h)
              * (ML_QK_DIM ** -0.5)).astype(BF16)
    ng = ng_ref[...]

    for hd in range(ML_HEADS):
        q0 = hd * ML_QK_DIM
        v0 = ML_QK_W + hd * ML_V_DIM
        o0 = ML_QK_W + ML_V_W + hd * ML_V_DIM
        qt = qt_all[q0:q0 + ML_QK_DIM]
        if hd % 2 == 0:
            vt_pair = _dot_nt(wfm_ref[v0:v0 + 2 * ML_V_DIM, :], h).astype(BF16)
        vt = vt_pair[(hd % 2) * ML_V_DIM:(hd % 2 + 1) * ML_V_DIM]
        ot = _dot_nt(wfm_ref[o0:o0 + ML_V_DIM, :], h)
        kh = k_all[:, q0:q0 + ML_QK_DIM]

        d_log = jnp.where(key_before_query,
                          r_tm[:, hd:hd + 1] + b_fm[hd:hd + 1, :], -jnp.inf)
        inter_h = inter[hd:hd + 1, :]
        m_t = jnp.maximum(inter_h, jnp.max(d_log, axis=0, keepdims=True))
        p = jnp.exp(d_log - m_t)
        sc = _dot(kh, qt) * p
        a = jnp.exp(inter_h - m_t)
        st_old = state_ref[hd]
        cq = _dot(st_old.astype(BF16), qt)
        num = a * cq[0:ML_V_DIM] + _dot(vt, sc.astype(BF16))
        den = a * cq[ML_V_DIM:ML_V_DIM + 1] + jnp.sum(sc, axis=0, keepdims=True)
        hc = num * (1.0 / jnp.maximum(jnp.abs(den), jnp.exp(-m_t)))

        w_row = w_fm[hd:hd + 1, :]
        n_rows = jnp.where(
            lax.broadcasted_iota(jnp.int32, (2 * V7X_SUBLANES, L), 0) == 0,
            w_row, 0.0)
        vw = jnp.concatenate([vt.astype(F32) * w_row, n_rows],
                             axis=0).astype(BF16)
        state_ref[hd] = decay[hd:hd + 1, 0:ML_QK_DIM] * st_old + _dot(vw, kh)

        hn = hc * lax.rsqrt(jnp.mean(hc * hc, axis=0, keepdims=True) + EPS)
        ng_h = _lane_tile(ng[hd * ML_V_DIM:(hd + 1) * ML_V_DIM, :], reps)
        y_ref[hd * ML_V_DIM:(hd + 1) * ML_V_DIM, :] = (
            hn * ng_h * jax.nn.sigmoid(ot)).astype(BF16)

    out = _dot_tn(y_ref[...], wout_ref[...])
    return x + (1.0 + mod[2:3]) * out


def _mlstm_layer(x, mods, layer, norm_g, w_fm, w_k, w_gt, gate_bt, ng_fm,
                 w_out, j):
    bsz, s, d = x.shape
    L = ML_SEQ_TILE
    tok = lambda b, i: (b, i, 0)
    per_j = lambda b, i: (j, 0, 0)
    return pl.pallas_call(
        _mlstm_kernel,
        grid=(bsz, s // L),
        in_specs=[
            pl.BlockSpec((None, L, d), tok),
            pl.BlockSpec((None, None, 6, d), lambda b, i: (layer, b, 0, 0)),
            pl.BlockSpec((None, 1, d), lambda b, i: (layer, 0, 0)),
            _resident((None, ML_FM_ROWS, d), per_j),
            _resident((None, d, ML_QK_W), per_j),
            _resident((None, ML_GATE_ROWS, d), per_j),
            pl.BlockSpec((None, ML_GATE_ROWS, V7X_LANES), per_j),
            _resident((None, ML_V_W, V7X_LANES), per_j),
            _resident((None, ML_V_W, d), per_j),
        ],
        out_specs=pl.BlockSpec((None, L, d), tok),
        out_shape=jax.ShapeDtypeStruct(x.shape, F32),
        scratch_shapes=[
            pltpu.VMEM((ML_HEADS, ML_STATE_ROWS, ML_QK_DIM), F32),
            pltpu.VMEM((V7X_SUBLANES, V7X_LANES), F32),
            pltpu.VMEM((L // ML_CHUNK, ML_V_W, ML_CHUNK), BF16),
        ],
        compiler_params=_compiler_params(),
        name=f"mlstm_layer{layer}",
    )(x, mods, norm_g, w_fm, w_k, w_gt, gate_bt, ng_fm, w_out)


def _mlstm_weights(ml_w_in, ml_b_i, ml_b_f, ml_norm_g):
    n_ml = ml_w_in.shape[0]
    q, k, v, o, gi, gf = jnp.split(
        ml_w_in, [ML_QK_W, 2 * ML_QK_W, 2 * ML_QK_W + ML_V_W,
                  2 * ML_QK_W + 2 * ML_V_W, 2 * ML_QK_W + 2 * ML_V_W + ML_HEADS],
        axis=-1)
    w_fm = jnp.concatenate([q, v, o], axis=-1).transpose(0, 2, 1).astype(BF16)
    w_k = k.astype(BF16)
    row_pad = V7X_SUBLANES - ML_HEADS
    pad_rows = lambda t: jnp.pad(t, ((0, 0), (0, row_pad), (0, 0)))
    w_gt = jnp.concatenate([pad_rows(gi.transpose(0, 2, 1)),
                            pad_rows(gf.transpose(0, 2, 1))], axis=1).astype(BF16)
    bt = jnp.concatenate([pad_rows(ml_b_i[:, :, None]),
                          pad_rows(ml_b_f[:, :, None])], axis=1)
    gate_bt = jnp.broadcast_to(bt, (n_ml, ML_GATE_ROWS, V7X_LANES))
    ng_fm = jnp.broadcast_to(ml_norm_g[:, :, None], (n_ml, ML_V_W, V7X_LANES))
    return w_fm, w_k, w_gt, gate_bt, ng_fm


def kernel(x, c, ada_w, ada_b, norm_mix_g, norm_ffn_g, sc_w_in, sc_conv_w, sc_w_out, ml_w_in, ml_b_i, ml_b_f, ml_norm_g, ml_w_out, ffn_w_up, ffn_conv_w, ffn_conv_b, ffn_w_down, final_norm_g):
    mods = _ada_modulation(c, ada_w, ada_b)

    sc_w_in_b = sc_w_in.astype(BF16)
    sc_w_out_b = sc_w_out.astype(BF16)
    ml_weights = _mlstm_weights(ml_w_in, ml_b_i, ml_b_f, ml_norm_g)
    ml_w_out_b = ml_w_out.astype(BF16)
    ffn_w_up_b = ffn_w_up.astype(BF16)
    ffn_w_down_b = ffn_w_down.astype(BF16)
    norm_mix_g = norm_mix_g.reshape(DEPTH, 1, D_MODEL)
    norm_ffn_g = norm_ffn_g.reshape(DEPTH, 1, D_MODEL)

    for layer in range(DEPTH):
        j = layer // 2
        if layer % 2 == 0:
            x = _sconv_layer(x, mods, layer, norm_mix_g, sc_w_in_b, sc_conv_w,
                             sc_w_out_b, j)
        else:
            x = _mlstm_layer(x, mods, layer, norm_mix_g, *ml_weights,
                             ml_w_out_b, j)
        x = _ffn_layer(x, mods, layer, norm_ffn_g, ffn_w_up_b, ffn_conv_w,
                       ffn_conv_b, ffn_w_down_b, final_norm_g,
                       final_norm=(layer == DEPTH - 1))
    return x
```

```python
import functools

import jax
import jax.numpy as jnp
from jax import lax
from jax.experimental import pallas as pl
from jax.experimental.pallas import tpu as pltpu

D_MODEL = 1024
DEPTH = 4
CONV_WIDTH = 3
ML_HEADS = 4
ML_V_DIM = D_MODEL // ML_HEADS
ML_QK_DIM = ML_V_DIM // 2
ML_QK_W = ML_HEADS * ML_QK_DIM
ML_V_W = ML_HEADS * ML_V_DIM
GATE_SOFTCAP = 15.0
D_FF = 2816
EPS = 1e-6

V7X_LANES = 128
V7X_SUBLANES = 8
V7X_MXU_DIM = 256
V7X_VMEM_BYTES = 64 * 1024 * 1024
VMEM_LIMIT_BYTES = V7X_VMEM_BYTES - 8 * 1024 * 1024

ML_CHUNK = 256
ML_SEQ_TILE = 1024
FFN_SEQ_TILE = 512
SCONV_SEQ_TILE = 1024
SCONV_STREAM_ROWS = 256
FFN_STREAM_ROWS = 512
FF_CHUNK = V7X_MXU_DIM
SC_CHUNK = V7X_MXU_DIM
ADA_TILE = 1536

BF16 = jnp.bfloat16
F32 = jnp.float32


def _dot(a, b):
    return jnp.dot(a, b, preferred_element_type=F32)


def _norm_modulate(x, g, shift, scale):
    ms = jnp.mean(x * x, axis=-1, keepdims=True)
    y = x * lax.rsqrt(ms + EPS)
    return (y * g) * (1.0 + scale) + shift


def _carry_halo(hist_ref, t):
    halo = V7X_SUBLANES
    hist_ref[0:halo, :] = hist_ref[t:t + halo, :]


def _causal_conv3(p, hist_ref, r0, cols, w):
    n = p.shape[0]
    base = V7X_SUBLANES + r0
    hist_ref[base:base + n, cols] = p
    p1 = hist_ref[base - 1:base - 1 + n, cols]
    p2 = hist_ref[base - 2:base - 2 + n, cols]
    return w[0:1] * p2 + w[1:2] * p1 + w[2:3] * p


def _resident(block_shape, index_map):
    return pl.BlockSpec(block_shape, index_map, pipeline_mode=pl.Buffered(1))


def _compiler_params():
    return pltpu.CompilerParams(
        dimension_semantics=("arbitrary", "arbitrary"),
        vmem_limit_bytes=VMEM_LIMIT_BYTES)


def _ada_kernel(c_ref, w_ref, b_ref, o_ref):
    c = c_ref[...]
    cond = (c * jax.nn.sigmoid(c)).astype(BF16)
    o_ref[...] = _dot(cond, w_ref[...].astype(BF16)) + b_ref[...]


def _ada_modulation(c, ada_w, ada_b):
    bsz, d = c.shape
    n = ada_w.shape[-1]
    out = pl.pallas_call(
        _ada_kernel,
        grid=(DEPTH, n // ADA_TILE),
        in_specs=[
            pl.BlockSpec((bsz, d), lambda l, j: (0, 0)),
            pl.BlockSpec((None, d, ADA_TILE), lambda l, j: (l, 0, j)),
            pl.BlockSpec((None, 1, ADA_TILE), lambda l, j: (l, 0, j)),
        ],
        out_specs=pl.BlockSpec((None, bsz, ADA_TILE), lambda l, j: (l, 0, j)),
        out_shape=jax.ShapeDtypeStruct((DEPTH, bsz, n), F32),
        compiler_params=_compiler_params(),
        name="adaln_mod",
    )(c, ada_w, ada_b.reshape(DEPTH, 1, n))
    return out.reshape(DEPTH, bsz, 6, d)


def _sconv_kernel(x_ref, mod_ref, g_ref, win_ref, cw_ref, wout_ref, o_ref,
                  hist_ref, y_ref):
    t = x_ref.shape[0]

    @pl.when(pl.program_id(1) == 0)
    def _():
        hist_ref[t:, :] = jnp.zeros((V7X_SUBLANES, D_MODEL), F32)

    _carry_halo(hist_ref, t)
    mod = mod_ref[...]
    for r0 in range(0, t, SCONV_STREAM_ROWS):
        rows = slice(r0, r0 + SCONV_STREAM_ROWS)
        xs = x_ref[rows, :]
        h = _norm_modulate(xs, g_ref[...], mod[0:1], mod[1:2]).astype(BF16)
        for j in range(D_MODEL // SC_CHUNK):
            lo = j * SC_CHUNK
            gb = _dot(h, win_ref[:, lo:lo + SC_CHUNK])
            gc = _dot(h, win_ref[:, D_MODEL + lo:D_MODEL + lo + SC_CHUNK])
            u = _dot(h, win_ref[:, 2 * D_MODEL + lo:2 * D_MODEL + lo + SC_CHUNK])
            conv = _causal_conv3(gc * u, hist_ref, r0, slice(lo, lo + SC_CHUNK),
                                 cw_ref[:, lo:lo + SC_CHUNK])
            y_ref[rows, lo:lo + SC_CHUNK] = (gb * conv).astype(BF16)
        out = _dot(y_ref[rows, :], wout_ref[...])
        o_ref[rows, :] = xs + (1.0 + mod[2:3]) * out


def _sconv_layer(x, mods, layer, norm_g, w_in, conv_w, w_out, j):
    bsz, s, d = x.shape
    tm = SCONV_SEQ_TILE
    return pl.pallas_call(
        _sconv_kernel,
        grid=(bsz, s // tm),
        in_specs=[
            pl.BlockSpec((None, tm, d), lambda b, i: (b, i, 0)),
            pl.BlockSpec((None, None, 6, d), lambda b, i: (layer, b, 0, 0)),
            pl.BlockSpec((None, 1, d), lambda b, i: (layer, 0, 0)),
            _resident((None, d, 3 * d), lambda b, i: (j, 0, 0)),
            pl.BlockSpec((None, CONV_WIDTH, d), lambda b, i: (j, 0, 0)),
            _resident((None, d, d), lambda b, i: (j, 0, 0)),
        ],
        out_specs=pl.BlockSpec((None, tm, d), lambda b, i: (b, i, 0)),
        out_shape=jax.ShapeDtypeStruct(x.shape, F32),
        scratch_shapes=[
            pltpu.VMEM((tm + V7X_SUBLANES, d), F32),
            pltpu.VMEM((tm, d), BF16),
        ],
        compiler_params=_compiler_params(),
        name=f"sconv_layer{layer}",
    )(x, mods, norm_g, w_in, conv_w, w_out)


def _ffn_kernel(x_ref, mod_ref, g_ref, wup_ref, cw_ref, cb_ref, wdown_ref,
                fg_ref, o_ref, hist_ref, a_ref, *, final_norm):
    t = x_ref.shape[0]

    @pl.when(pl.program_id(1) == 0)
    def _():
        hist_ref[t:, :] = jnp.zeros((V7X_SUBLANES, 2 * D_FF), F32)

    _carry_halo(hist_ref, t)
    mod = mod_ref[...]
    for r0 in range(0, t, FFN_STREAM_ROWS):
        rows = slice(r0, r0 + FFN_STREAM_ROWS)
        xs = x_ref[rows, :]
        h = _norm_modulate(xs, g_ref[...], mod[3:4], mod[4:5]).astype(BF16)
        for j in range(D_FF // FF_CHUNK):
            halves = []
            for base in (0, D_FF):
                lo = base + j * FF_CHUNK
                cols = slice(lo, lo + FF_CHUNK)
                p = _dot(h, wup_ref[:, cols])
                halves.append(
                    _causal_conv3(p, hist_ref, r0, cols, cw_ref[:, cols])
                    + cb_ref[:, cols])
            gate, val = halves
            act = gate * jax.nn.sigmoid(gate) * val
            a_ref[rows, j * FF_CHUNK:(j + 1) * FF_CHUNK] = act.astype(BF16)
        out = _dot(a_ref[rows, :], wdown_ref[...])
        y = xs + (1.0 + mod[5:6]) * out
        if final_norm:
            ms = jnp.mean(y * y, axis=-1, keepdims=True)
            y = y * lax.rsqrt(ms + EPS) * fg_ref[...]
        o_ref[rows, :] = y


def _ffn_layer(x, mods, layer, norm_g, w_up, conv_w, conv_b, w_down, final_g,
               final_norm):
    bsz, s, d = x.shape
    tm = FFN_SEQ_TILE
    f2 = 2 * D_FF
    return pl.pallas_call(
        functools.partial(_ffn_kernel, final_norm=final_norm),
        grid=(bsz, s // tm),
        in_specs=[
            pl.BlockSpec((None, tm, d), lambda b, i: (b, i, 0)),
            pl.BlockSpec((None, None, 6, d), lambda b, i: (layer, b, 0, 0)),
            pl.BlockSpec((None, 1, d), lambda b, i: (layer, 0, 0)),
            _resident((None, d, f2), lambda b, i: (layer, 0, 0)),
            pl.BlockSpec((None, CONV_WIDTH, f2), lambda b, i: (layer, 0, 0)),
            pl.BlockSpec((None, 1, f2), lambda b, i: (layer, 0, 0)),
            _resident((None, D_FF, d), lambda b, i: (layer, 0, 0)),
            pl.BlockSpec((1, d), lambda b, i: (0, 0)),
        ],
        out_specs=pl.BlockSpec((None, tm, d), lambda b, i: (b, i, 0)),
        out_shape=jax.ShapeDtypeStruct(x.shape, F32),
        scratch_shapes=[
            pltpu.VMEM((tm + V7X_SUBLANES, f2), F32),
            pltpu.VMEM((tm, D_FF), BF16),
        ],
        compiler_params=_compiler_params(),
        name=f"ffn_layer{layer}",
    )(x, mods, norm_g, w_up, conv_w, conv_b.reshape(DEPTH, 1, f2), w_down,
      final_g.reshape(1, d))


ML_STATE_ROWS = ML_V_DIM + 16
ML_FM_ROWS = ML_QK_W + 2 * ML_V_W
ML_GATE_ROWS = 2 * V7X_SUBLANES


def _log_sigmoid(z):
    return jnp.minimum(z, 0.0) - jnp.log1p(jnp.exp(-jnp.abs(z)))


def _softcap(z):
    return GATE_SOFTCAP * jnp.tanh(z * (1.0 / GATE_SOFTCAP))


def _split3(z):
    hi = z.astype(BF16).astype(F32)
    r1 = z - hi
    mid = r1.astype(BF16).astype(F32)
    lo = (r1 - mid).astype(BF16).astype(F32)
    return hi, mid, lo


def _lane_tile(a, reps):
    return jnp.concatenate([a] * reps, axis=1)


def _dot_nt(a, b):
    return lax.dot_general(a, b, (((1,), (1,)), ((), ())),
                           preferred_element_type=F32)


def _dot_tn(a, b):
    return lax.dot_general(a, b, (((0,), (0,)), ((), ())),
                           preferred_element_type=F32)


def _mlstm_kernel(x_ref, mod_ref, g_ref, wfm_ref, wk_ref, wgt_ref,
                  gbt_ref, ng_ref, wout_ref, out_ref,
                  state_ref, m_ref, y_ref):
    @pl.when(pl.program_id(1) == 0)
    def _():
        state_ref[...] = jnp.zeros_like(state_ref)
        m_ref[...] = jnp.zeros_like(m_ref)

    for c in range(x_ref.shape[0] // ML_CHUNK):
        rows = slice(c * ML_CHUNK, (c + 1) * ML_CHUNK)
        out_ref[rows, :] = _mlstm_chunk(
            x_ref[rows, :], mod_ref[...], g_ref, wfm_ref, wk_ref, wgt_ref,
            gbt_ref, ng_ref, wout_ref, state_ref, m_ref, y_ref.at[c])


def _mlstm_chunk(x, mod, g_ref, wfm_ref, wk_ref, wgt_ref, gbt_ref, ng_ref,
                 wout_ref, state_ref, m_ref, y_ref):
    L = x.shape[0]
    reps = L // V7X_LANES
    h = _norm_modulate(x, g_ref[...], mod[0:1], mod[1:2]).astype(BF16)

    row = lax.broadcasted_iota(jnp.int32, (L, L), 0)
    col = lax.broadcasted_iota(jnp.int32, (L, L), 1)
    upper = jnp.where(row <= col, 1.0, 0.0).astype(BF16)
    key_before_query = row <= col

    g_fm = _dot_nt(wgt_ref[...], h) + _lane_tile(gbt_ref[...], reps)
    i_fm = _softcap(g_fm[0:V7X_SUBLANES])
    f_fm = _log_sigmoid(_softcap(g_fm[V7X_SUBLANES:]))
    hi, mid, lo = _split3(f_fm)
    stacked = jnp.concatenate([hi, mid, lo, jnp.zeros_like(hi)], axis=0)
    cs = _dot(stacked.astype(BF16), upper)
    b_fm = cs[0:8] + cs[8:16] + cs[16:24]
    r_fm = i_fm - b_fm
    r_tm = jnp.concatenate(
        [r_fm, jnp.zeros((V7X_LANES - V7X_SUBLANES, L), F32)], axis=0).T
    m_prev = _lane_tile(m_ref[...], reps)
    b_last = jnp.broadcast_to(b_fm[:, L - 1:L], (V7X_SUBLANES, L))
    inter = b_fm + m_prev
    m_new = b_last + jnp.maximum(m_prev, jnp.max(r_fm, axis=1, keepdims=True))
    decay = jnp.exp(b_last + m_prev - m_new)
    w_fm = jnp.exp(b_last + r_fm - m_new)
    m_ref[...] = m_new[:, 0:V7X_LANES]

    k_all = _dot(h, wk_ref[...]).astype(BF16)
    qt_all = (_dot_nt(wfm_ref[0:ML_QK_W, :], h)
              * (ML_QK_DIM ** -0.5)).astype(BF16)
    ng = ng_ref[...]

    for hd in range(ML_HEADS):
        q0 = hd * ML_QK_DIM
        v0 = ML_QK_W + hd * ML_V_DIM
        o0 = ML_QK_W + ML_V_W + hd * ML_V_DIM
        qt = qt_all[q0:q0 + ML_QK_DIM]
        if hd % 2 == 0:
            vt_pair = _dot_nt(wfm_ref[v0:v0 + 2 * ML_V_DIM, :], h).astype(BF16)
        vt = vt_pair[(hd % 2) * ML_V_DIM:(hd % 2 + 1) * ML_V_DIM]
        ot = _dot_nt(wfm_ref[o0:o0 + ML_V_DIM, :], h)
        kh = k_all[:, q0:q0 + ML_QK_DIM]

        d_log = jnp.where(key_before_query,
                          r_tm[:, hd:hd + 1] + b_fm[hd:hd + 1, :], -jnp.inf)
        inter_h = inter[hd:hd + 1, :]
        m_t = jnp.maximum(inter_h, jnp.max(d_log, axis=0, keepdims=True))
        p = jnp.exp(d_log - m_t)
        sc = _dot(kh, qt) * p
        a = jnp.exp(inter_h - m_t)
        st_old = state_ref[hd]
        cq = _dot(st_old.astype(BF16), qt)
        num = a * cq[0:ML_V_DIM] + _dot(vt, sc.astype(BF16))
        den = a * cq[ML_V_DIM:ML_V_DIM + 1] + jnp.sum(sc, axis=0, keepdims=True)
        hc = num * (1.0 / jnp.maximum(jnp.abs(den), jnp.exp(-m_t)))

        w_row = w_fm[hd:hd + 1, :]
        n_rows = jnp.where(
            lax.broadcasted_iota(jnp.int32, (2 * V7X_SUBLANES, L), 0) == 0,
            w_row, 0.0)
        vw = jnp.concatenate([vt.astype(F32) * w_row, n_rows],
                             axis=0).astype(BF16)
        state_ref[hd] = decay[hd:hd + 1, 0:ML_QK_DIM] * st_old + _dot(vw, kh)

        hn = hc * lax.rsqrt(jnp.mean(hc * hc, axis=0, keepdims=True) + EPS)
        ng_h = _lane_tile(ng[hd * ML_V_DIM:(hd + 1) * ML_V_DIM, :], reps)
        y_ref[hd * ML_V_DIM:(hd + 1) * ML_V_DIM, :] = (
            hn * ng_h * jax.nn.sigmoid(ot)).astype(BF16)

    out = _dot_tn(y_ref[...], wout_ref[...])
    return x + (1.0 + mod[2:3]) * out


def _mlstm_layer(x, mods, layer, norm_g, w_fm, w_k, w_gt, gate_bt, ng_fm,
                 w_out, j):
    bsz, s, d = x.shape
    L = ML_SEQ_TILE
    tok = lambda b, i: (b, i, 0)
    per_j = lambda b, i: (j, 0, 0)
    return pl.pallas_call(
        _mlstm_kernel,
        grid=(bsz, s // L),
        in_specs=[
            pl.BlockSpec((None, L, d), tok),
            pl.BlockSpec((None, None, 6, d), lambda b, i: (layer, b, 0, 0)),
            pl.BlockSpec((None, 1, d), lambda b, i: (layer, 0, 0)),
            _resident((None, ML_FM_ROWS, d), per_j),
            _resident((None, d, ML_QK_W), per_j),
            _resident((None, ML_GATE_ROWS, d), per_j),
            pl.BlockSpec((None, ML_GATE_ROWS, V7X_LANES), per_j),
            _resident((None, ML_V_W, V7X_LANES), per_j),
            _resident((None, ML_V_W, d), per_j),
        ],
        out_specs=pl.BlockSpec((None, L, d), tok),
        out_shape=jax.ShapeDtypeStruct(x.shape, F32),
        scratch_shapes=[
            pltpu.VMEM((ML_HEADS, ML_STATE_ROWS, ML_QK_DIM), F32),
            pltpu.VMEM((V7X_SUBLANES, V7X_LANES), F32),
            pltpu.VMEM((L // ML_CHUNK, ML_V_W, ML_CHUNK), BF16),
        ],
        compiler_params=_compiler_params(),
        name=f"mlstm_layer{layer}",
    )(x, mods, norm_g, w_fm, w_k, w_gt, gate_bt, ng_fm, w_out)


def _mlstm_weights(ml_w_in, ml_b_i, ml_b_f, ml_norm_g):
    n_ml = ml_w_in.shape[0]
    q, k, v, o, gi, gf = jnp.split(
        ml_w_in, [ML_QK_W, 2 * ML_QK_W, 2 * ML_QK_W + ML_V_W,
                  2 * ML_QK_W + 2 * ML_V_W, 2 * ML_QK_W + 2 * ML_V_W + ML_HEADS],
        axis=-1)
    w_fm = jnp.concatenate([q, v, o], axis=-1).transpose(0, 2, 1).astype(BF16)
    w_k = k.astype(BF16)
    row_pad = V7X_SUBLANES - ML_HEADS
    pad_rows = lambda t: jnp.pad(t, ((0, 0), (0, row_pad), (0, 0)))
    w_gt = jnp.concatenate([pad_rows(gi.transpose(0, 2, 1)),
                            pad_rows(gf.transpose(0, 2, 1))], axis=1).astype(BF16)
    bt = jnp.concatenate([pad_rows(ml_b_i[:, :, None]),
                          pad_rows(ml_b_f[:, :, None])], axis=1)
    gate_bt = jnp.broadcast_to(bt, (n_ml, ML_GATE_ROWS, V7X_LANES))
    ng_fm = jnp.broadcast_to(ml_norm_g[:, :, None], (n_ml, ML_V_W, V7X_LANES))
    return w_fm, w_k, w_gt, gate_bt, ng_fm


def kernel(x, c, ada_w, ada_b, norm_mix_g, norm_ffn_g, sc_w_in, sc_conv_w, sc_w_out, ml_w_in, ml_b_i, ml_b_f, ml_norm_g, ml_w_out, ffn_w_up, ffn_conv_w, ffn_conv_b, ffn_w_down, final_norm_g):
    mods = _ada_modulation(c, ada_w, ada_b)

    sc_w_in_b = sc_w_in.astype(BF16)
    sc_w_out_b = sc_w_out.astype(BF16)
    ml_weights = _mlstm_weights(ml_w_in, ml_b_i, ml_b_f, ml_norm_g)
    ml_w_out_b = ml_w_out.astype(BF16)
    ffn_w_up_b = ffn_w_up.astype(BF16)
    ffn_w_down_b = ffn_w_down.astype(BF16)
    norm_mix_g = norm_mix_g.reshape(DEPTH, 1, D_MODEL)
    norm_ffn_g = norm_ffn_g.reshape(DEPTH, 1, D_MODEL)

    for layer in range(DEPTH):
        j = layer // 2
        if layer % 2 == 0:
            x = _sconv_layer(x, mods, layer, norm_mix_g, sc_w_in_b, sc_conv_w,
                             sc_w_out_b, j)
        else:
            x = _mlstm_layer(x, mods, layer, norm_mix_g, *ml_weights,
                             ml_w_out_b, j)
        x = _ffn_layer(x, mods, layer, norm_ffn_g, ffn_w_up_b, ffn_conv_w,
                       ffn_conv_b, ffn_w_down_b, final_norm_g,
                       final_norm=(layer == DEPTH - 1))
    return x
```

```python
import functools

import jax
import jax.numpy as jnp
from jax import lax
from jax.experimental import pallas as pl
from jax.experimental.pallas import tpu as pltpu

D_MODEL = 1024
DEPTH = 4
CONV_WIDTH = 3
ML_HEADS = 4
ML_V_DIM = D_MODEL // ML_HEADS
ML_QK_DIM = ML_V_DIM // 2
ML_QK_W = ML_HEADS * ML_QK_DIM
ML_V_W = ML_HEADS * ML_V_DIM
GATE_SOFTCAP = 15.0
D_FF = 2816
EPS = 1e-6

V7X_LANES = 128
V7X_SUBLANES = 8
V7X_MXU_DIM = 256
V7X_VMEM_BYTES = 64 * 1024 * 1024
COMPILER_SCRATCH_BYTES = 8 * 1024 * 1024
VMEM_LIMIT_BYTES = V7X_VMEM_BYTES - COMPILER_SCRATCH_BYTES

ML_CHUNK = 256
ML_SEQ_TILE = 1024
FFN_SEQ_TILE = 512
SCONV_SEQ_TILE = 1024
SCONV_STREAM_ROWS = 256
FFN_STREAM_ROWS = 512
FF_CHUNK = V7X_MXU_DIM
SC_CHUNK = V7X_MXU_DIM
ADA_TILE = 1536

BF16 = jnp.bfloat16
F32 = jnp.float32


def _dot(a, b):
    return jnp.dot(a, b, preferred_element_type=F32)


def _norm_modulate(x, g, shift, scale):
    ms = jnp.mean(x * x, axis=-1, keepdims=True)
    y = x * lax.rsqrt(ms + EPS)
    return (y * g) * (1.0 + scale) + shift


def _carry_halo(hist_ref, t):
    halo = V7X_SUBLANES
    hist_ref[0:halo, :] = hist_ref[t:t + halo, :]


def _causal_conv3(p, hist_ref, r0, cols, w):
    n = p.shape[0]
    base = V7X_SUBLANES + r0
    hist_ref[base:base + n, cols] = p
    p1 = hist_ref[base - 1:base - 1 + n, cols]
    p2 = hist_ref[base - 2:base - 2 + n, cols]
    return w[0:1] * p2 + w[1:2] * p1 + w[2:3] * p


def _resident(block_shape, index_map):
    return pl.BlockSpec(block_shape, index_map, pipeline_mode=pl.Buffered(1))


def _compiler_params():
    return pltpu.CompilerParams(
        dimension_semantics=("arbitrary", "arbitrary"),
        vmem_limit_bytes=VMEM_LIMIT_BYTES)


def _ada_kernel(c_ref, w_ref, b_ref, o_ref):
    c = c_ref[...]
    cond = (c * jax.nn.sigmoid(c)).astype(BF16)
    o_ref[...] = _dot(cond, w_ref[...].astype(BF16)) + b_ref[...]


def _ada_modulation(c, ada_w, ada_b):
    bsz, d = c.shape
    n = ada_w.shape[-1]
    out = pl.pallas_call(
        _ada_kernel,
        grid=(DEPTH, n // ADA_TILE),
        in_specs=[
            pl.BlockSpec((bsz, d), lambda l, j: (0, 0)),
            pl.BlockSpec((None, d, ADA_TILE), lambda l, j: (l, 0, j)),
            pl.BlockSpec((None, 1, ADA_TILE), lambda l, j: (l, 0, j)),
        ],
        out_specs=pl.BlockSpec((None, bsz, ADA_TILE), lambda l, j: (l, 0, j)),
        out_shape=jax.ShapeDtypeStruct((DEPTH, bsz, n), F32),
        compiler_params=_compiler_params(),
        name="adaln_mod",
    )(c, ada_w, ada_b.reshape(DEPTH, 1, n))
    return out.reshape(DEPTH, bsz, 6, d)


def _sconv_kernel(x_ref, mod_ref, g_ref, win_ref, cw_ref, wout_ref, o_ref,
                  hist_ref, y_ref):
    t = x_ref.shape[0]

    @pl.when(pl.program_id(1) == 0)
    def _():
        hist_ref[t:, :] = jnp.zeros((V7X_SUBLANES, D_MODEL), F32)

    _carry_halo(hist_ref, t)
    mod = mod_ref[...]
    for r0 in range(0, t, SCONV_STREAM_ROWS):
        rows = slice(r0, r0 + SCONV_STREAM_ROWS)
        xs = x_ref[rows, :]
        h = _norm_modulate(xs, g_ref[...], mod[0:1], mod[1:2]).astype(BF16)
        for j in range(D_MODEL // SC_CHUNK):
            lo = j * SC_CHUNK
            gb = _dot(h, win_ref[:, lo:lo + SC_CHUNK])
            gc = _dot(h, win_ref[:, D_MODEL + lo:D_MODEL + lo + SC_CHUNK])
            u = _dot(h, win_ref[:, 2 * D_MODEL + lo:2 * D_MODEL + lo + SC_CHUNK])
            conv = _causal_conv3(gc * u, hist_ref, r0, slice(lo, lo + SC_CHUNK),
                                 cw_ref[:, lo:lo + SC_CHUNK])
            y_ref[rows, lo:lo + SC_CHUNK] = (gb * conv).astype(BF16)
        out = _dot(y_ref[rows, :], wout_ref[...])
        o_ref[rows, :] = xs + (1.0 + mod[2:3]) * out


def _sconv_layer(x, mods, layer, norm_g, w_in, conv_w, w_out, j):
    bsz, s, d = x.shape
    tm = SCONV_SEQ_TILE
    return pl.pallas_call(
        _sconv_kernel,
        grid=(bsz, s // tm),
        in_specs=[
            pl.BlockSpec((None, tm, d), lambda b, i: (b, i, 0)),
            pl.BlockSpec((None, None, 6, d), lambda b, i: (layer, b, 0, 0)),
            pl.BlockSpec((None, 1, d), lambda b, i: (layer, 0, 0)),
            _resident((None, d, 3 * d), lambda b, i: (j, 0, 0)),
            pl.BlockSpec((None, CONV_WIDTH, d), lambda b, i: (j, 0, 0)),
            _resident((None, d, d), lambda b, i: (j, 0, 0)),
        ],
        out_specs=pl.BlockSpec((None, tm, d), lambda b, i: (b, i, 0)),
        out_shape=jax.ShapeDtypeStruct(x.shape, F32),
        scratch_shapes=[
            pltpu.VMEM((tm + V7X_SUBLANES, d), F32),
            pltpu.VMEM((tm, d), BF16),
        ],
        compiler_params=_compiler_params(),
        name=f"sconv_layer{layer}",
    )(x, mods, norm_g, w_in, conv_w, w_out)


def _ffn_kernel(x_ref, mod_ref, g_ref, wup_ref, cw_ref, cb_ref, wdown_ref,
                fg_ref, o_ref, hist_ref, a_ref, *, final_norm):
    t = x_ref.shape[0]

    @pl.when(pl.program_id(1) == 0)
    def _():
        hist_ref[t:, :] = jnp.zeros((V7X_SUBLANES, 2 * D_FF), F32)

    _carry_halo(hist_ref, t)
    mod = mod_ref[...]
    for r0 in range(0, t, FFN_STREAM_ROWS):
        rows = slice(r0, r0 + FFN_STREAM_ROWS)
        xs = x_ref[rows, :]
        h = _norm_modulate(xs, g_ref[...], mod[3:4], mod[4:5]).astype(BF16)
        for j in range(D_FF // FF_CHUNK):
            halves = []
            for base in (0, D_FF):
                lo = base + j * FF_CHUNK
                cols = slice(lo, lo + FF_CHUNK)
                p = _dot(h, wup_ref[:, cols])
                halves.append(
                    _causal_conv3(p, hist_ref, r0, cols, cw_ref[:, cols])
                    + cb_ref[:, cols])
            gate, val = halves
            act = gate * jax.nn.sigmoid(gate) * val
            a_ref[rows, j * FF_CHUNK:(j + 1) * FF_CHUNK] = act.astype(BF16)
        out = _dot(a_ref[rows, :], wdown_ref[...])
        y = xs + (1.0 + mod[5:6]) * out
        if final_norm:
            ms = jnp.mean(y * y, axis=-1, keepdims=True)
            y = y * lax.rsqrt(ms + EPS) * fg_ref[...]
        o_ref[rows, :] = y


def _ffn_layer(x, mods, layer, norm_g, w_up, conv_w, conv_b, w_down, final_g,
               final_norm):
    bsz, s, d = x.shape
    tm = FFN_SEQ_TILE
    f2 = 2 * D_FF
    return pl.pallas_call(
        functools.partial(_ffn_kernel, final_norm=final_norm),
        grid=(bsz, s // tm),
        in_specs=[
            pl.BlockSpec((None, tm, d), lambda b, i: (b, i, 0)),
            pl.BlockSpec((None, None, 6, d), lambda b, i: (layer, b, 0, 0)),
            pl.BlockSpec((None, 1, d), lambda b, i: (layer, 0, 0)),
            _resident((None, d, f2), lambda b, i: (layer, 0, 0)),
            pl.BlockSpec((None, CONV_WIDTH, f2), lambda b, i: (layer, 0, 0)),
            pl.BlockSpec((None, 1, f2), lambda b, i: (layer, 0, 0)),
            _resident((None, D_FF, d), lambda b, i: (layer, 0, 0)),
            pl.BlockSpec((1, d), lambda b, i: (0, 0)),
        ],
        out_specs=pl.BlockSpec((None, tm, d), lambda b, i: (b, i, 0)),
        out_shape=jax.ShapeDtypeStruct(x.shape, F32),
        scratch_shapes=[
            pltpu.VMEM((tm + V7X_SUBLANES, f2), F32),
            pltpu.VMEM((tm, D_FF), BF16),
        ],
        compiler_params=_compiler_params(),
        name=f"ffn_layer{layer}",
    )(x, mods, norm_g, w_up, conv_w, conv_b.reshape(DEPTH, 1, f2), w_down,
      final_g.reshape(1, d))


ML_STATE_ROWS = ML_V_DIM + 16
ML_FM_ROWS = ML_QK_W + 2 * ML_V_W
ML_GATE_ROWS = 2 * V7X_SUBLANES


def _log_sigmoid(z):
    return jnp.minimum(z, 0.0) - jnp.log1p(jnp.exp(-jnp.abs(z)))


def _softcap(z):
    return GATE_SOFTCAP * jnp.tanh(z * (1.0 / GATE_SOFTCAP))


def _split3(z):
    hi = z.astype(BF16).astype(F32)
    r1 = z - hi
    mid = r1.astype(BF16).astype(F32)
    lo = (r1 - mid).astype(BF16).astype(F32)
    return hi, mid, lo


def _lane_tile(a, reps):
    return jnp.concatenate([a] * reps, axis=1)


def _dot_nt(a, b):
    return lax.dot_general(a, b, (((1,), (1,)), ((), ())),
                           preferred_element_type=F32)


def _dot_tn(a, b):
    return lax.dot_general(a, b, (((0,), (0,)), ((), ())),
                           preferred_element_type=F32)


def _mlstm_kernel(x_ref, mod_ref, g_ref, wfm_ref, wk_ref, wgt_ref,
                  gbt_ref, ng_ref, wout_ref, out_ref,
                  state_ref, m_ref, y_ref):
    @pl.when(pl.program_id(1) == 0)
    def _():
        state_ref[...] = jnp.zeros_like(state_ref)
        m_ref[...] = jnp.zeros_like(m_ref)

    row = lax.broadcasted_iota(jnp.int32, (ML_CHUNK, ML_CHUNK), 0)
    col = lax.broadcasted_iota(jnp.int32, (ML_CHUNK, ML_CHUNK), 1)
    key_before_query = row <= col
    upper = jnp.where(key_before_query, 1.0, 0.0).astype(BF16)

    for c in range(x_ref.shape[0] // ML_CHUNK):
        rows = slice(c * ML_CHUNK, (c + 1) * ML_CHUNK)
        out_ref[rows, :] = _mlstm_chunk(
            x_ref[rows, :], mod_ref[...], key_before_query, upper, g_ref,
            wfm_ref, wk_ref, wgt_ref, gbt_ref, ng_ref, wout_ref, state_ref,
            m_ref, y_ref.at[c])


def _mlstm_chunk(x, mod, key_before_query, upper, g_ref, wfm_ref, wk_ref,
                 wgt_ref, gbt_ref, ng_ref, wout_ref, state_ref, m_ref, y_ref):
    L = x.shape[0]
    reps = L // V7X_LANES
    h = _norm_modulate(x, g_ref[...], mod[0:1], mod[1:2]).astype(BF16)

    g_fm = _dot_nt(wgt_ref[...], h) + _lane_tile(gbt_ref[...], reps)
    i_fm = _softcap(g_fm[0:V7X_SUBLANES])
    f_fm = _log_sigmoid(_softcap(g_fm[V7X_SUBLANES:]))
    hi, mid, lo = _split3(f_fm)
    stacked = jnp.concatenate([hi, mid, lo, jnp.zeros_like(hi)], axis=0)
    cs = _dot(stacked.astype(BF16), upper)
    b_fm = cs[0:8] + cs[8:16] + cs[16:24]
    r_fm = i_fm - b_fm
    r_tm = jnp.concatenate(
        [r_fm, jnp.zeros((V7X_LANES - V7X_SUBLANES, L), F32)], axis=0).T
    m_prev = _lane_tile(m_ref[...], reps)
    b_last = jnp.broadcast_to(b_fm[:, L - 1:L], (V7X_SUBLANES, L))
    inter = b_fm + m_prev
    m_new = b_last + jnp.maximum(m_prev, jnp.max(r_fm, axis=1, keepdims=True))
    decay = jnp.exp(b_last + m_prev - m_new)
    w_fm = jnp.exp(b_last + r_fm - m_new)
    m_ref[...] = m_new[:, 0:V7X_LANES]

    k_all = _dot(h, wk_ref[...]).astype(BF16)
    qt_all = (_dot_nt(wfm_ref[0:ML_QK_W, :], h)
              * (ML_QK_DIM ** -0.5)).astype(BF16)
    ng = ng_ref[...]

    for hd in range(ML_HEADS):
        q0 = hd * ML_QK_DIM
        v0 = ML_QK_W + hd * ML_V_DIM
        o0 = ML_QK_W + ML_V_W + hd * ML_V_DIM
        qt = qt_all[q0:q0 + ML_QK_DIM]
        if hd % 2 == 0:
            vt_pair = _dot_nt(wfm_ref[v0:v0 + 2 * ML_V_DIM, :], h).astype(BF16)
        vt = vt_pair[(hd % 2) * ML_V_DIM:(hd % 2 + 1) * ML_V_DIM]
        ot = _dot_nt(wfm_ref[o0:o0 + ML_V_DIM, :], h)
        kh = k_all[:, q0:q0 + ML_QK_DIM]

        d_log = jnp.where(key_before_query,
                          r_tm[:, hd:hd + 1] + b_fm[hd:hd + 1, :], -jnp.inf)
        inter_h = inter[hd:hd + 1, :]
        m_t = jnp.maximum(inter_h, jnp.max(d_log, axis=0, keepdims=True))
        p = jnp.exp(d_log - m_t)
        sc = _dot(kh, qt) * p
        a = jnp.exp(inter_h - m_t)
        st_old = state_ref[hd]
        cq = _dot(st_old.astype(BF16), qt)
        num = a * cq[0:ML_V_DIM] + _dot(vt, sc.astype(BF16))
        den = a * cq[ML_V_DIM:ML_V_DIM + 1] + jnp.sum(sc, axis=0, keepdims=True)
        hc = num * (1.0 / jnp.maximum(jnp.abs(den), jnp.exp(-m_t)))

        w_row = w_fm[hd:hd + 1, :]
        n_rows = jnp.where(
            lax.broadcasted_iota(jnp.int32, (2 * V7X_SUBLANES, L), 0) == 0,
            w_row, 0.0)
        vw = jnp.concatenate([vt.astype(F32) * w_row, n_rows],
                             axis=0).astype(BF16)
        state_ref[hd] = decay[hd:hd + 1, 0:ML_QK_DIM] * st_old + _dot(vw, kh)

        hn = hc * lax.rsqrt(jnp.mean(hc * hc, axis=0, keepdims=True) + EPS)
        ng_h = _lane_tile(ng[hd * ML_V_DIM:(hd + 1) * ML_V_DIM, :], reps)
        y_ref[hd * ML_V_DIM:(hd + 1) * ML_V_DIM, :] = (
            hn * ng_h * jax.nn.sigmoid(ot)).astype(BF16)

    out = _dot_tn(y_ref[...], wout_ref[...])
    return x + (1.0 + mod[2:3]) * out


def _mlstm_layer(x, mods, layer, norm_g, w_fm, w_k, w_gt, gate_bt, ng_fm,
                 w_out, j):
    bsz, s, d = x.shape
    L = ML_SEQ_TILE
    tok = lambda b, i: (b, i, 0)
    per_j = lambda b, i: (j, 0, 0)
    return pl.pallas_call(
        _mlstm_kernel,
        grid=(bsz, s // L),
        in_specs=[
            pl.BlockSpec((None, L, d), tok),
            pl.BlockSpec((None, None, 6, d), lambda b, i: (layer, b, 0, 0)),
            pl.BlockSpec((None, 1, d), lambda b, i: (layer, 0, 0)),
            _resident((None, ML_FM_ROWS, d), per_j),
            _resident((None, d, ML_QK_W), per_j),
            _resident((None, ML_GATE_ROWS, d), per_j),
            pl.BlockSpec((None, ML_GATE_ROWS, V7X_LANES), per_j),
            _resident((None, ML_V_W, V7X_LANES), per_j),
            _resident((None, ML_V_W, d), per_j),
        ],
        out_specs=pl.BlockSpec((None, L, d), tok),
        out_shape=jax.ShapeDtypeStruct(x.shape, F32),
        scratch_shapes=[
            pltpu.VMEM((ML_HEADS, ML_STATE_ROWS, ML_QK_DIM), F32),
            pltpu.VMEM((V7X_SUBLANES, V7X_LANES), F32),
            pltpu.VMEM((L // ML_CHUNK, ML_V_W, ML_CHUNK), BF16),
        ],
        compiler_params=_compiler_params(),
        name=f"mlstm_layer{layer}",
    )(x, mods, norm_g, w_fm, w_k, w_gt, gate_bt, ng_fm, w_out)


def _mlstm_weights(ml_w_in, ml_b_i, ml_b_f, ml_norm_g):
    n_ml = ml_w_in.shape[0]
    q, k, v, o, gi, gf = jnp.split(
        ml_w_in, [ML_QK_W, 2 * ML_QK_W, 2 * ML_QK_W + ML_V_W,
                  2 * ML_QK_W + 2 * ML_V_W, 2 * ML_QK_W + 2 * ML_V_W + ML_HEADS],
        axis=-1)
    w_fm = jnp.concatenate([q, v, o], axis=-1).transpose(0, 2, 1).astype(BF16)
    w_k = k.astype(BF16)
    row_pad = V7X_SUBLANES - ML_HEADS
    pad_rows = lambda t: jnp.pad(t, ((0, 0), (0, row_pad), (0, 0)))
    w_gt = jnp.concatenate([pad_rows(gi.transpose(0, 2, 1)),
                            pad_rows(gf.transpose(0, 2, 1))], axis=1).astype(BF16)
    bt = jnp.concatenate([pad_rows(ml_b_i[:, :, None]),
                          pad_rows(ml_b_f[:, :, None])], axis=1)
    gate_bt = jnp.broadcast_to(bt, (n_ml, ML_GATE_ROWS, V7X_LANES))
    ng_fm = jnp.broadcast_to(ml_norm_g[:, :, None], (n_ml, ML_V_W, V7X_LANES))
    return w_fm, w_k, w_gt, gate_bt, ng_fm


def kernel(x, c, ada_w, ada_b, norm_mix_g, norm_ffn_g, sc_w_in, sc_conv_w, sc_w_out, ml_w_in, ml_b_i, ml_b_f, ml_norm_g, ml_w_out, ffn_w_up, ffn_conv_w, ffn_conv_b, ffn_w_down, final_norm_g):
    mods = _ada_modulation(c, ada_w, ada_b)

    sc_w_in_b = sc_w_in.astype(BF16)
    sc_w_out_b = sc_w_out.astype(BF16)
    ml_weights = _mlstm_weights(ml_w_in, ml_b_i, ml_b_f, ml_norm_g)
    ml_w_out_b = ml_w_out.astype(BF16)
    ffn_w_up_b = ffn_w_up.astype(BF16)
    ffn_w_down_b = ffn_w_down.astype(BF16)
    norm_mix_g = norm_mix_g.reshape(DEPTH, 1, D_MODEL)
    norm_ffn_g = norm_ffn_g.reshape(DEPTH, 1, D_MODEL)

    for layer in range(DEPTH):
        j = layer // 2
        if layer % 2 == 0:
            x = _sconv_layer(x, mods, layer, norm_mix_g, sc_w_in_b, sc_conv_w,
                             sc_w_out_b, j)
        else:
            x = _mlstm_layer(x, mods, layer, norm_mix_g, *ml_weights,
                             ml_w_out_b, j)
        x = _ffn_layer(x, mods, layer, norm_ffn_g, ffn_w_up_b, ffn_conv_w,
                       ffn_conv_b, ffn_w_down_b, final_norm_g,
                       final_norm=(layer == DEPTH - 1))
    return x
```

```python
import functools

import jax
import jax.numpy as jnp
from jax import lax
from jax.experimental import pallas as pl
from jax.experimental.pallas import tpu as pltpu

D_MODEL = 1024
DEPTH = 4
CONV_WIDTH = 3
ML_HEADS = 4
ML_V_DIM = D_MODEL // ML_HEADS
ML_QK_DIM = ML_V_DIM // 2
ML_QK_W = ML_HEADS * ML_QK_DIM
ML_V_W = ML_HEADS * ML_V_DIM
GATE_SOFTCAP = 15.0
D_FF = 2816
EPS = 1e-6

V7X_LANES = 128
V7X_SUBLANES = 8
V7X_MXU_DIM = 256
V7X_VMEM_BYTES = 64 * 1024 * 1024
COMPILER_SCRATCH_BYTES = 8 * 1024 * 1024
VMEM_LIMIT_BYTES = V7X_VMEM_BYTES - COMPILER_SCRATCH_BYTES

ML_CHUNK = 256
ML_SEQ_TILE = 2048
FFN_SEQ_TILE = 512
SCONV_SEQ_TILE = 1024
SCONV_STREAM_ROWS = 512
FFN_STREAM_ROWS = 512
FF_CHUNK = V7X_MXU_DIM
SC_CHUNK = V7X_MXU_DIM
ADA_TILE = 1536

BF16 = jnp.bfloat16
F32 = jnp.float32


def _dot(a, b):
    return jnp.dot(a, b, preferred_element_type=F32)


def _norm_modulate(x, g, shift, scale):
    ms = jnp.mean(x * x, axis=-1, keepdims=True)
    y = x * lax.rsqrt(ms + EPS)
    return (y * g) * (1.0 + scale) + shift


def _carry_halo(hist_ref, t):
    halo = V7X_SUBLANES
    hist_ref[0:halo, :] = hist_ref[t:t + halo, :]


def _causal_conv3(p, hist_ref, r0, cols, w):
    n = p.shape[0]
    base = V7X_SUBLANES + r0
    hist_ref[base:base + n, cols] = p
    p1 = hist_ref[base - 1:base - 1 + n, cols]
    p2 = hist_ref[base - 2:base - 2 + n, cols]
    return w[0:1] * p2 + w[1:2] * p1 + w[2:3] * p


def _resident(block_shape, index_map):
    return pl.BlockSpec(block_shape, index_map, pipeline_mode=pl.Buffered(1))


def _compiler_params():
    return pltpu.CompilerParams(
        dimension_semantics=("arbitrary", "arbitrary"),
        vmem_limit_bytes=VMEM_LIMIT_BYTES)


def _ada_kernel(c_ref, w_ref, b_ref, o_ref):
    c = c_ref[...]
    cond = (c * jax.nn.sigmoid(c)).astype(BF16)
    o_ref[...] = _dot(cond, w_ref[...].astype(BF16)) + b_ref[...]


def _ada_modulation(c, ada_w, ada_b):
    bsz, d = c.shape
    n = ada_w.shape[-1]
    out = pl.pallas_call(
        _ada_kernel,
        grid=(DEPTH, n // ADA_TILE),
        in_specs=[
            pl.BlockSpec((bsz, d), lambda l, j: (0, 0)),
            pl.BlockSpec((None, d, ADA_TILE), lambda l, j: (l, 0, j)),
            pl.BlockSpec((None, 1, ADA_TILE), lambda l, j: (l, 0, j)),
        ],
        out_specs=pl.BlockSpec((None, bsz, ADA_TILE), lambda l, j: (l, 0, j)),
        out_shape=jax.ShapeDtypeStruct((DEPTH, bsz, n), F32),
        compiler_params=_compiler_params(),
        name="adaln_mod",
    )(c, ada_w, ada_b.reshape(DEPTH, 1, n))
    return out.reshape(DEPTH, bsz, 6, d)


def _sconv_kernel(x_ref, mod_ref, g_ref, win_ref, cw_ref, wout_ref, o_ref,
                  hist_ref, y_ref):
    t = x_ref.shape[0]

    @pl.when(pl.program_id(1) == 0)
    def _():
        hist_ref[t:, :] = jnp.zeros((V7X_SUBLANES, D_MODEL), F32)

    _carry_halo(hist_ref, t)
    mod = mod_ref[...]
    for r0 in range(0, t, SCONV_STREAM_ROWS):
        rows = slice(r0, r0 + SCONV_STREAM_ROWS)
        xs = x_ref[rows, :]
        h = _norm_modulate(xs, g_ref[...], mod[0:1], mod[1:2]).astype(BF16)
        for j in range(D_MODEL // SC_CHUNK):
            lo = j * SC_CHUNK
            gb = _dot(h, win_ref[:, lo:lo + SC_CHUNK])
            gc = _dot(h, win_ref[:, D_MODEL + lo:D_MODEL + lo + SC_CHUNK])
            u = _dot(h, win_ref[:, 2 * D_MODEL + lo:2 * D_MODEL + lo + SC_CHUNK])
            conv = _causal_conv3(gc * u, hist_ref, r0, slice(lo, lo + SC_CHUNK),
                                 cw_ref[:, lo:lo + SC_CHUNK])
            y_ref[rows, lo:lo + SC_CHUNK] = (gb * conv).astype(BF16)
        out = _dot(y_ref[rows, :], wout_ref[...])
        o_ref[rows, :] = xs + (1.0 + mod[2:3]) * out


def _sconv_layer(x, mods, layer, norm_g, w_in, conv_w, w_out, j):
    bsz, s, d = x.shape
    tm = SCONV_SEQ_TILE
    return pl.pallas_call(
        _sconv_kernel,
        grid=(bsz, s // tm),
        in_specs=[
            pl.BlockSpec((None, tm, d), lambda b, i: (b, i, 0)),
            pl.BlockSpec((None, None, 6, d), lambda b, i: (layer, b, 0, 0)),
            pl.BlockSpec((None, 1, d), lambda b, i: (layer, 0, 0)),
            _resident((None, d, 3 * d), lambda b, i: (j, 0, 0)),
            pl.BlockSpec((None, CONV_WIDTH, d), lambda b, i: (j, 0, 0)),
            _resident((None, d, d), lambda b, i: (j, 0, 0)),
        ],
        out_specs=pl.BlockSpec((None, tm, d), lambda b, i: (b, i, 0)),
        out_shape=jax.ShapeDtypeStruct(x.shape, F32),
        scratch_shapes=[
            pltpu.VMEM((tm + V7X_SUBLANES, d), F32),
            pltpu.VMEM((tm, d), BF16),
        ],
        compiler_params=_compiler_params(),
        name=f"sconv_layer{layer}",
    )(x, mods, norm_g, w_in, conv_w, w_out)


def _ffn_kernel(x_ref, mod_ref, g_ref, wup_ref, cw_ref, cb_ref, wdown_ref,
                fg_ref, o_ref, hist_ref, a_ref, *, final_norm):
    t = x_ref.shape[0]

    @pl.when(pl.program_id(1) == 0)
    def _():
        hist_ref[t:, :] = jnp.zeros((V7X_SUBLANES, 2 * D_FF), F32)

    _carry_halo(hist_ref, t)
    mod = mod_ref[...]
    for r0 in range(0, t, FFN_STREAM_ROWS):
        rows = slice(r0, r0 + FFN_STREAM_ROWS)
        xs = x_ref[rows, :]
        h = _norm_modulate(xs, g_ref[...], mod[3:4], mod[4:5]).astype(BF16)
        for j in range(D_FF // FF_CHUNK):
            halves = []
            for base in (0, D_FF):
                lo = base + j * FF_CHUNK
                cols = slice(lo, lo + FF_CHUNK)
                p = _dot(h, wup_ref[:, cols])
                halves.append(
                    _causal_conv3(p, hist_ref, r0, cols, cw_ref[:, cols])
                    + cb_ref[:, cols])
            gate, val = halves
            act = gate * jax.nn.sigmoid(gate) * val
            a_ref[rows, j * FF_CHUNK:(j + 1) * FF_CHUNK] = act.astype(BF16)
        out = _dot(a_ref[rows, :], wdown_ref[...])
        y = xs + (1.0 + mod[5:6]) * out
        if final_norm:
            ms = jnp.mean(y * y, axis=-1, keepdims=True)
            y = y * lax.rsqrt(ms + EPS) * fg_ref[...]
        o_ref[rows, :] = y


def _ffn_layer(x, mods, layer, norm_g, w_up, conv_w, conv_b, w_down, final_g,
               final_norm):
    bsz, s, d = x.shape
    tm = FFN_SEQ_TILE
    f2 = 2 * D_FF
    return pl.pallas_call(
        functools.partial(_ffn_kernel, final_norm=final_norm),
        grid=(bsz, s // tm),
        in_specs=[
            pl.BlockSpec((None, tm, d), lambda b, i: (b, i, 0)),
            pl.BlockSpec((None, None, 6, d), lambda b, i: (layer, b, 0, 0)),
            pl.BlockSpec((None, 1, d), lambda b, i: (layer, 0, 0)),
            _resident((None, d, f2), lambda b, i: (layer, 0, 0)),
            pl.BlockSpec((None, CONV_WIDTH, f2), lambda b, i: (layer, 0, 0)),
            pl.BlockSpec((None, 1, f2), lambda b, i: (layer, 0, 0)),
            _resident((None, D_FF, d), lambda b, i: (layer, 0, 0)),
            pl.BlockSpec((1, d), lambda b, i: (0, 0)),
        ],
        out_specs=pl.BlockSpec((None, tm, d), lambda b, i: (b, i, 0)),
        out_shape=jax.ShapeDtypeStruct(x.shape, F32),
        scratch_shapes=[
            pltpu.VMEM((tm + V7X_SUBLANES, f2), F32),
            pltpu.VMEM((tm, D_FF), BF16),
        ],
        compiler_params=_compiler_params(),
        name=f"ffn_layer{layer}",
    )(x, mods, norm_g, w_up, conv_w, conv_b.reshape(DEPTH, 1, f2), w_down,
      final_g.reshape(1, d))


ML_STATE_ROWS = ML_V_DIM + 16
ML_FM_ROWS = ML_QK_W + 2 * ML_V_W
ML_GATE_ROWS = 2 * V7X_SUBLANES


def _log_sigmoid(z):
    return jnp.minimum(z, 0.0) - jnp.log1p(jnp.exp(-jnp.abs(z)))


def _softcap(z):
    return GATE_SOFTCAP * jnp.tanh(z * (1.0 / GATE_SOFTCAP))


def _split3(z):
    hi = z.astype(BF16).astype(F32)
    r1 = z - hi
    mid = r1.astype(BF16).astype(F32)
    lo = (r1 - mid).astype(BF16).astype(F32)
    return hi, mid, lo


def _lane_tile(a, reps):
    return jnp.concatenate([a] * reps, axis=1)


def _dot_nt(a, b):
    return lax.dot_general(a, b, (((1,), (1,)), ((), ())),
                           preferred_element_type=F32)


def _dot_tn(a, b):
    return lax.dot_general(a, b, (((0,), (0,)), ((), ())),
                           preferred_element_type=F32)


def _mlstm_kernel(x_ref, mod_ref, g_ref, wfm_ref, wk_ref, wgt_ref,
                  gbt_ref, ng_ref, wout_ref, out_ref,
                  state_ref, m_ref, y_ref):
    @pl.when(pl.program_id(1) == 0)
    def _():
        state_ref[...] = jnp.zeros_like(state_ref)
        m_ref[...] = jnp.zeros_like(m_ref)

    row = lax.broadcasted_iota(jnp.int32, (ML_CHUNK, ML_CHUNK), 0)
    col = lax.broadcasted_iota(jnp.int32, (ML_CHUNK, ML_CHUNK), 1)
    key_before_query = row <= col
    upper = jnp.where(key_before_query, 1.0, 0.0).astype(BF16)

    for c in range(x_ref.shape[0] // ML_CHUNK):
        rows = slice(c * ML_CHUNK, (c + 1) * ML_CHUNK)
        out_ref[rows, :] = _mlstm_chunk(
            x_ref[rows, :], mod_ref[...], key_before_query, upper, g_ref,
            wfm_ref, wk_ref, wgt_ref, gbt_ref, ng_ref, wout_ref, state_ref,
            m_ref, y_ref.at[c])


def _mlstm_chunk(x, mod, key_before_query, upper, g_ref, wfm_ref, wk_ref,
                 wgt_ref, gbt_ref, ng_ref, wout_ref, state_ref, m_ref, y_ref):
    L = x.shape[0]
    reps = L // V7X_LANES
    h = _norm_modulate(x, g_ref[...], mod[0:1], mod[1:2]).astype(BF16)

    g_fm = _dot_nt(wgt_ref[...], h) + _lane_tile(gbt_ref[...], reps)
    i_fm = _softcap(g_fm[0:V7X_SUBLANES])
    f_fm = _log_sigmoid(_softcap(g_fm[V7X_SUBLANES:]))
    hi, mid, lo = _split3(f_fm)
    stacked = jnp.concatenate([hi, mid, lo, jnp.zeros_like(hi)], axis=0)
    cs = _dot(stacked.astype(BF16), upper)
    b_fm = cs[0:8] + cs[8:16] + cs[16:24]
    r_fm = i_fm - b_fm
    r_tm = jnp.concatenate(
        [r_fm, jnp.zeros((V7X_LANES - V7X_SUBLANES, L), F32)], axis=0).T
    m_prev = _lane_tile(m_ref[...], reps)
    b_last = jnp.broadcast_to(b_fm[:, L - 1:L], (V7X_SUBLANES, L))
    inter = b_fm + m_prev
    m_new = b_last + jnp.maximum(m_prev, jnp.max(r_fm, axis=1, keepdims=True))
    decay = jnp.exp(b_last + m_prev - m_new)
    w_fm = jnp.exp(b_last + r_fm - m_new)
    m_ref[...] = m_new[:, 0:V7X_LANES]

    k_all = _dot(h, wk_ref[...]).astype(BF16)
    qt_all = (_dot_nt(wfm_ref[0:ML_QK_W, :], h)
              * (ML_QK_DIM ** -0.5)).astype(BF16)
    ng = ng_ref[...]

    for hd in range(ML_HEADS):
        q0 = hd * ML_QK_DIM
        v0 = ML_QK_W + hd * ML_V_DIM
        o0 = ML_QK_W + ML_V_W + hd * ML_V_DIM
        qt = qt_all[q0:q0 + ML_QK_DIM]
        if hd % 2 == 0:
            vt_pair = _dot_nt(wfm_ref[v0:v0 + 2 * ML_V_DIM, :], h).astype(BF16)
        vt = vt_pair[(hd % 2) * ML_V_DIM:(hd % 2 + 1) * ML_V_DIM]
        ot = _dot_nt(wfm_ref[o0:o0 + ML_V_DIM, :], h)
        kh = k_all[:, q0:q0 + ML_QK_DIM]

        d_log = jnp.where(key_before_query,
                          r_tm[:, hd:hd + 1] + b_fm[hd:hd + 1, :], -jnp.inf)
        inter_h = inter[hd:hd + 1, :]
        m_t = jnp.maximum(inter_h, jnp.max(d_log, axis=0, keepdims=True))
        p = jnp.exp(d_log - m_t)
        sc = _dot(kh, qt) * p
        a = jnp.exp(inter_h - m_t)
        st_old = state_ref[hd]
        cq = _dot(st_old.astype(BF16), qt)
        num = a * cq[0:ML_V_DIM] + _dot(vt, sc.astype(BF16))
        den = a * cq[ML_V_DIM:ML_V_DIM + 1] + jnp.sum(sc, axis=0, keepdims=True)
        hc = num * (1.0 / jnp.maximum(jnp.abs(den), jnp.exp(-m_t)))

        w_row = w_fm[hd:hd + 1, :]
        n_rows = jnp.where(
            lax.broadcasted_iota(jnp.int32, (2 * V7X_SUBLANES, L), 0) == 0,
            w_row, 0.0)
        vw = jnp.concatenate([vt.astype(F32) * w_row, n_rows],
                             axis=0).astype(BF16)
        state_ref[hd] = decay[hd:hd + 1, 0:ML_QK_DIM] * st_old + _dot(vw, kh)

        hn = hc * lax.rsqrt(jnp.mean(hc * hc, axis=0, keepdims=True) + EPS)
        ng_h = _lane_tile(ng[hd * ML_V_DIM:(hd + 1) * ML_V_DIM, :], reps)
        y_ref[hd * ML_V_DIM:(hd + 1) * ML_V_DIM, :] = (
            hn * ng_h * jax.nn.sigmoid(ot)).astype(BF16)

    out = _dot_tn(y_ref[...], wout_ref[...])
    return x + (1.0 + mod[2:3]) * out


def _mlstm_layer(x, mods, layer, norm_g, w_fm, w_k, w_gt, gate_bt, ng_fm,
                 w_out, j):
    bsz, s, d = x.shape
    L = ML_SEQ_TILE
    tok = lambda b, i: (b, i, 0)
    per_j = lambda b, i: (j, 0, 0)
    return pl.pallas_call(
        _mlstm_kernel,
        grid=(bsz, s // L),
        in_specs=[
            pl.BlockSpec((None, L, d), tok),
            pl.BlockSpec((None, None, 6, d), lambda b, i: (layer, b, 0, 0)),
            pl.BlockSpec((None, 1, d), lambda b, i: (layer, 0, 0)),
            _resident((None, ML_FM_ROWS, d), per_j),
            _resident((None, d, ML_QK_W), per_j),
            _resident((None, ML_GATE_ROWS, d), per_j),
            pl.BlockSpec((None, ML_GATE_ROWS, V7X_LANES), per_j),
            _resident((None, ML_V_W, V7X_LANES), per_j),
            _resident((None, ML_V_W, d), per_j),
        ],
        out_specs=pl.BlockSpec((None, L, d), tok),
        out_shape=jax.ShapeDtypeStruct(x.shape, F32),
        scratch_shapes=[
            pltpu.VMEM((ML_HEADS, ML_STATE_ROWS, ML_QK_DIM), F32),
            pltpu.VMEM((V7X_SUBLANES, V7X_LANES), F32),
            pltpu.VMEM((L // ML_CHUNK, ML_V_W, ML_CHUNK), BF16),
        ],
        compiler_params=_compiler_params(),
        name=f"mlstm_layer{layer}",
    )(x, mods, norm_g, w_fm, w_k, w_gt, gate_bt, ng_fm, w_out)


def _mlstm_weights(ml_w_in, ml_b_i, ml_b_f, ml_norm_g):
    n_ml = ml_w_in.shape[0]
    q, k, v, o, gi, gf = jnp.split(
        ml_w_in, [ML_QK_W, 2 * ML_QK_W, 2 * ML_QK_W + ML_V_W,
                  2 * ML_QK_W + 2 * ML_V_W, 2 * ML_QK_W + 2 * ML_V_W + ML_HEADS],
        axis=-1)
    w_fm = jnp.concatenate([q, v, o], axis=-1).transpose(0, 2, 1).astype(BF16)
    w_k = k.astype(BF16)
    row_pad = V7X_SUBLANES - ML_HEADS
    pad_rows = lambda t: jnp.pad(t, ((0, 0), (0, row_pad), (0, 0)))
    w_gt = jnp.concatenate([pad_rows(gi.transpose(0, 2, 1)),
                            pad_rows(gf.transpose(0, 2, 1))], axis=1).astype(BF16)
    bt = jnp.concatenate([pad_rows(ml_b_i[:, :, None]),
                          pad_rows(ml_b_f[:, :, None])], axis=1)
    gate_bt = jnp.broadcast_to(bt, (n_ml, ML_GATE_ROWS, V7X_LANES))
    ng_fm = jnp.broadcast_to(ml_norm_g[:, :, None], (n_ml, ML_V_W, V7X_LANES))
    return w_fm, w_k, w_gt, gate_bt, ng_fm


def kernel(x, c, ada_w, ada_b, norm_mix_g, norm_ffn_g, sc_w_in, sc_conv_w, sc_w_out, ml_w_in, ml_b_i, ml_b_f, ml_norm_g, ml_w_out, ffn_w_up, ffn_conv_w, ffn_conv_b, ffn_w_down, final_norm_g):
    mods = _ada_modulation(c, ada_w, ada_b)

    sc_w_in_b = sc_w_in.astype(BF16)
    sc_w_out_b = sc_w_out.astype(BF16)
    ml_weights = _mlstm_weights(ml_w_in, ml_b_i, ml_b_f, ml_norm_g)
    ml_w_out_b = ml_w_out.astype(BF16)
    ffn_w_up_b = ffn_w_up.astype(BF16)
    ffn_w_down_b = ffn_w_down.astype(BF16)
    norm_mix_g = norm_mix_g.reshape(DEPTH, 1, D_MODEL)
    norm_ffn_g = norm_ffn_g.reshape(DEPTH, 1, D_MODEL)

    for layer in range(DEPTH):
        j = layer // 2
        if layer % 2 == 0:
            x = _sconv_layer(x, mods, layer, norm_mix_g, sc_w_in_b, sc_conv_w,
                             sc_w_out_b, j)
        else:
            x = _mlstm_layer(x, mods, layer, norm_mix_g, *ml_weights,
                             ml_w_out_b, j)
        x = _ffn_layer(x, mods, layer, norm_ffn_g, ffn_w_up_b, ffn_conv_w,
                       ffn_conv_b, ffn_w_down_b, final_norm_g,
                       final_norm=(layer == DEPTH - 1))
    return x
```

```python
import functools

import jax
import jax.numpy as jnp
from jax import lax
from jax.experimental import pallas as pl
from jax.experimental.pallas import tpu as pltpu

D_MODEL = 1024
DEPTH = 4
CONV_WIDTH = 3
ML_HEADS = 4
ML_V_DIM = D_MODEL // ML_HEADS
ML_QK_DIM = ML_V_DIM // 2
ML_QK_W = ML_HEADS * ML_QK_DIM
ML_V_W = ML_HEADS * ML_V_DIM
GATE_SOFTCAP = 15.0
D_FF = 2816
EPS = 1e-6

V7X_LANES = 128
V7X_SUBLANES = 8
V7X_MXU_DIM = 256
V7X_VMEM_BYTES = 64 * 1024 * 1024
COMPILER_SCRATCH_BYTES = 8 * 1024 * 1024
VMEM_LIMIT_BYTES = V7X_VMEM_BYTES - COMPILER_SCRATCH_BYTES

ML_CHUNK = 256
ML_SEQ_TILE = 2048
FFN_SEQ_TILE = 512
SCONV_SEQ_TILE = 1024
SCONV_STREAM_ROWS = 512
FFN_STREAM_ROWS = 512
FF_CHUNK = V7X_MXU_DIM
SC_CHUNK = V7X_MXU_DIM
ADA_TILE = 1536

BF16 = jnp.bfloat16
F32 = jnp.float32


def _dot(a, b):
    return jnp.dot(a, b, preferred_element_type=F32)


def _norm_modulate(x, g, shift, scale):
    ms = jnp.mean(x * x, axis=-1, keepdims=True)
    y = x * lax.rsqrt(ms + EPS)
    return (y * g) * (1.0 + scale) + shift


def _carry_halo(hist_ref, t):
    halo = V7X_SUBLANES
    hist_ref[0:halo, :] = hist_ref[t:t + halo, :]


def _causal_conv3(p, hist_ref, r0, cols, w):
    n = p.shape[0]
    base = V7X_SUBLANES + r0
    hist_ref[base:base + n, cols] = p
    p1 = hist_ref[base - 1:base - 1 + n, cols]
    p2 = hist_ref[base - 2:base - 2 + n, cols]
    return w[0:1] * p2 + w[1:2] * p1 + w[2:3] * p


def _resident(block_shape, index_map):
    return pl.BlockSpec(block_shape, index_map, pipeline_mode=pl.Buffered(1))


def _compiler_params():
    return pltpu.CompilerParams(
        dimension_semantics=("arbitrary", "arbitrary"),
        vmem_limit_bytes=VMEM_LIMIT_BYTES)


def _ada_kernel(c_ref, w_ref, b_ref, o_ref):
    c = c_ref[...]
    cond = (c * jax.nn.sigmoid(c)).astype(BF16)
    o_ref[...] = _dot(cond, w_ref[...].astype(BF16)) + b_ref[...]


def _ada_modulation(c, ada_w, ada_b):
    bsz, d = c.shape
    n = ada_w.shape[-1]
    out = pl.pallas_call(
        _ada_kernel,
        grid=(DEPTH, n // ADA_TILE),
        in_specs=[
            pl.BlockSpec((bsz, d), lambda l, j: (0, 0)),
            pl.BlockSpec((None, d, ADA_TILE), lambda l, j: (l, 0, j)),
            pl.BlockSpec((None, 1, ADA_TILE), lambda l, j: (l, 0, j)),
        ],
        out_specs=pl.BlockSpec((None, bsz, ADA_TILE), lambda l, j: (l, 0, j)),
        out_shape=jax.ShapeDtypeStruct((DEPTH, bsz, n), F32),
        compiler_params=_compiler_params(),
        name="adaln_mod",
    )(c, ada_w, ada_b.reshape(DEPTH, 1, n))
    return out.reshape(DEPTH, bsz, 6, d)


def _sconv_kernel(x_ref, mod_ref, g_ref, win_ref, cw_ref, wout_ref, o_ref,
                  hist_ref, y_ref):
    t = x_ref.shape[0]

    @pl.when(pl.program_id(1) == 0)
    def _():
        hist_ref[t:, :] = jnp.zeros((V7X_SUBLANES, D_MODEL), F32)

    _carry_halo(hist_ref, t)
    mod = mod_ref[...]
    for r0 in range(0, t, SCONV_STREAM_ROWS):
        rows = slice(r0, r0 + SCONV_STREAM_ROWS)
        xs = x_ref[rows, :]
        h = _norm_modulate(xs, g_ref[...], mod[0:1], mod[1:2]).astype(BF16)
        for j in range(D_MODEL // SC_CHUNK):
            lo = j * SC_CHUNK
            gb = _dot(h, win_ref[:, lo:lo + SC_CHUNK])
            gc = _dot(h, win_ref[:, D_MODEL + lo:D_MODEL + lo + SC_CHUNK])
            u = _dot(h, win_ref[:, 2 * D_MODEL + lo:2 * D_MODEL + lo + SC_CHUNK])
            conv = _causal_conv3(gc * u, hist_ref, r0, slice(lo, lo + SC_CHUNK),
                                 cw_ref[:, lo:lo + SC_CHUNK])
            y_ref[rows, lo:lo + SC_CHUNK] = (gb * conv).astype(BF16)
        out = _dot(y_ref[rows, :], wout_ref[...])
        o_ref[rows, :] = xs + (1.0 + mod[2:3]) * out


def _sconv_layer(x, mods, layer, norm_g, w_in, conv_w, w_out, j):
    bsz, s, d = x.shape
    tm = SCONV_SEQ_TILE
    return pl.pallas_call(
        _sconv_kernel,
        grid=(bsz, s // tm),
        in_specs=[
            pl.BlockSpec((None, tm, d), lambda b, i: (b, i, 0)),
            pl.BlockSpec((None, None, 6, d), lambda b, i: (layer, b, 0, 0)),
            pl.BlockSpec((None, 1, d), lambda b, i: (layer, 0, 0)),
            _resident((None, d, 3 * d), lambda b, i: (j, 0, 0)),
            pl.BlockSpec((None, CONV_WIDTH, d), lambda b, i: (j, 0, 0)),
            _resident((None, d, d), lambda b, i: (j, 0, 0)),
        ],
        out_specs=pl.BlockSpec((None, tm, d), lambda b, i: (b, i, 0)),
        out_shape=jax.ShapeDtypeStruct(x.shape, F32),
        scratch_shapes=[
            pltpu.VMEM((tm + V7X_SUBLANES, d), F32),
            pltpu.VMEM((tm, d), BF16),
        ],
        compiler_params=_compiler_params(),
        name=f"sconv_layer{layer}",
    )(x, mods, norm_g, w_in, conv_w, w_out)


def _ffn_kernel(x_ref, mod_ref, g_ref, wup_ref, cw_ref, cb_ref, wdown_ref,
                fg_ref, o_ref, hist_ref, a_ref, *, final_norm):
    t = x_ref.shape[0]

    @pl.when(pl.program_id(1) == 0)
    def _():
        hist_ref[t:, :] = jnp.zeros((V7X_SUBLANES, 2 * D_FF), F32)

    _carry_halo(hist_ref, t)
    mod = mod_ref[...]
    for r0 in range(0, t, FFN_STREAM_ROWS):
        rows = slice(r0, r0 + FFN_STREAM_ROWS)
        xs = x_ref[rows, :]
        h = _norm_modulate(xs, g_ref[...], mod[3:4], mod[4:5]).astype(BF16)
        for j in range(D_FF // FF_CHUNK):
            cols = slice(2 * j * FF_CHUNK, 2 * (j + 1) * FF_CHUNK)
            p = _dot(h, wup_ref[:, cols])
            u = (_causal_conv3(p, hist_ref, r0, cols, cw_ref[:, cols])
                 + cb_ref[:, cols])
            gate, val = u[:, :FF_CHUNK], u[:, FF_CHUNK:]
            act = gate * jax.nn.sigmoid(gate) * val
            a_ref[rows, j * FF_CHUNK:(j + 1) * FF_CHUNK] = act.astype(BF16)
        out = _dot(a_ref[rows, :], wdown_ref[...])
        y = xs + (1.0 + mod[5:6]) * out
        if final_norm:
            ms = jnp.mean(y * y, axis=-1, keepdims=True)
            y = y * lax.rsqrt(ms + EPS) * fg_ref[...]
        o_ref[rows, :] = y


def _ffn_layer(x, mods, layer, norm_g, w_up, conv_w, conv_b, w_down, final_g,
               final_norm):
    bsz, s, d = x.shape
    tm = FFN_SEQ_TILE
    f2 = 2 * D_FF
    return pl.pallas_call(
        functools.partial(_ffn_kernel, final_norm=final_norm),
        grid=(bsz, s // tm),
        in_specs=[
            pl.BlockSpec((None, tm, d), lambda b, i: (b, i, 0)),
            pl.BlockSpec((None, None, 6, d), lambda b, i: (layer, b, 0, 0)),
            pl.BlockSpec((None, 1, d), lambda b, i: (layer, 0, 0)),
            _resident((None, d, f2), lambda b, i: (layer, 0, 0)),
            pl.BlockSpec((None, CONV_WIDTH, f2), lambda b, i: (layer, 0, 0)),
            pl.BlockSpec((None, 1, f2), lambda b, i: (layer, 0, 0)),
            _resident((None, D_FF, d), lambda b, i: (layer, 0, 0)),
            pl.BlockSpec((1, d), lambda b, i: (0, 0)),
        ],
        out_specs=pl.BlockSpec((None, tm, d), lambda b, i: (b, i, 0)),
        out_shape=jax.ShapeDtypeStruct(x.shape, F32),
        scratch_shapes=[
            pltpu.VMEM((tm + V7X_SUBLANES, f2), F32),
            pltpu.VMEM((tm, D_FF), BF16),
        ],
        compiler_params=_compiler_params(),
        name=f"ffn_layer{layer}",
    )(x, mods, norm_g, w_up, conv_w, conv_b.reshape(DEPTH, 1, f2), w_down,
      final_g.reshape(1, d))


ML_STATE_ROWS = ML_V_DIM + 16
ML_FM_ROWS = ML_QK_W + 2 * ML_V_W
ML_GATE_ROWS = 2 * V7X_SUBLANES


def _log_sigmoid(z):
    return jnp.minimum(z, 0.0) - jnp.log1p(jnp.exp(-jnp.abs(z)))


def _softcap(z):
    return GATE_SOFTCAP * jnp.tanh(z * (1.0 / GATE_SOFTCAP))


def _split3(z):
    hi = z.astype(BF16).astype(F32)
    r1 = z - hi
    mid = r1.astype(BF16).astype(F32)
    lo = (r1 - mid).astype(BF16).astype(F32)
    return hi, mid, lo


def _lane_tile(a, reps):
    return jnp.concatenate([a] * reps, axis=1)


def _dot_nt(a, b):
    return lax.dot_general(a, b, (((1,), (1,)), ((), ())),
                           preferred_element_type=F32)


def _dot_tn(a, b):
    return lax.dot_general(a, b, (((0,), (0,)), ((), ())),
                           preferred_element_type=F32)


def _mlstm_kernel(x_ref, mod_ref, g_ref, wfm_ref, wk_ref, wgt_ref,
                  gbt_ref, ng_ref, wout_ref, out_ref,
                  state_ref, m_ref, y_ref):
    @pl.when(pl.program_id(1) == 0)
    def _():
        state_ref[...] = jnp.zeros_like(state_ref)
        m_ref[...] = jnp.zeros_like(m_ref)

    row = lax.broadcasted_iota(jnp.int32, (ML_CHUNK, ML_CHUNK), 0)
    col = lax.broadcasted_iota(jnp.int32, (ML_CHUNK, ML_CHUNK), 1)
    key_before_query = row <= col
    upper = jnp.where(key_before_query, 1.0, 0.0).astype(BF16)

    for c in range(x_ref.shape[0] // ML_CHUNK):
        rows = slice(c * ML_CHUNK, (c + 1) * ML_CHUNK)
        out_ref[rows, :] = _mlstm_chunk(
            x_ref[rows, :], mod_ref[...], key_before_query, upper, g_ref,
            wfm_ref, wk_ref, wgt_ref, gbt_ref, ng_ref, wout_ref, state_ref,
            m_ref, y_ref.at[c])


def _mlstm_chunk(x, mod, key_before_query, upper, g_ref, wfm_ref, wk_ref,
                 wgt_ref, gbt_ref, ng_ref, wout_ref, state_ref, m_ref, y_ref):
    L = x.shape[0]
    reps = L // V7X_LANES
    h = _norm_modulate(x, g_ref[...], mod[0:1], mod[1:2]).astype(BF16)

    g_fm = _dot_nt(wgt_ref[...], h) + _lane_tile(gbt_ref[...], reps)
    i_fm = _softcap(g_fm[0:V7X_SUBLANES])
    f_fm = _log_sigmoid(_softcap(g_fm[V7X_SUBLANES:]))
    hi, mid, lo = _split3(f_fm)
    stacked = jnp.concatenate([hi, mid, lo, jnp.zeros_like(hi)], axis=0)
    cs = _dot(stacked.astype(BF16), upper)
    b_fm = cs[0:8] + cs[8:16] + cs[16:24]
    r_fm = i_fm - b_fm
    r_tm = jnp.concatenate(
        [r_fm, jnp.zeros((V7X_LANES - V7X_SUBLANES, L), F32)], axis=0).T
    m_prev = _lane_tile(m_ref[...], reps)
    b_last = jnp.broadcast_to(b_fm[:, L - 1:L], (V7X_SUBLANES, L))
    inter = b_fm + m_prev
    m_new = b_last + jnp.maximum(m_prev, jnp.max(r_fm, axis=1, keepdims=True))
    decay = jnp.exp(b_last + m_prev - m_new)
    w_fm = jnp.exp(b_last + r_fm - m_new)
    m_ref[...] = m_new[:, 0:V7X_LANES]

    k_all = _dot(h, wk_ref[...]).astype(BF16)
    qt_all = (_dot_nt(wfm_ref[0:ML_QK_W, :], h)
              * (ML_QK_DIM ** -0.5)).astype(BF16)
    ng = ng_ref[...]

    for hd in range(ML_HEADS):
        q0 = hd * ML_QK_DIM
        v0 = ML_QK_W + hd * ML_V_DIM
        o0 = ML_QK_W + ML_V_W + hd * ML_V_DIM
        qt = qt_all[q0:q0 + ML_QK_DIM]
        if hd % 2 == 0:
            vt_pair = _dot_nt(wfm_ref[v0:v0 + 2 * ML_V_DIM, :], h).astype(BF16)
        vt = vt_pair[(hd % 2) * ML_V_DIM:(hd % 2 + 1) * ML_V_DIM]
        ot = _dot_nt(wfm_ref[o0:o0 + ML_V_DIM, :], h)
        kh = k_all[:, q0:q0 + ML_QK_DIM]

        d_log = jnp.where(key_before_query,
                          r_tm[:, hd:hd + 1] + b_fm[hd:hd + 1, :], -jnp.inf)
        inter_h = inter[hd:hd + 1, :]
        m_t = jnp.maximum(inter_h, jnp.max(d_log, axis=0, keepdims=True))
        p = jnp.exp(d_log - m_t)
        sc = _dot(kh, qt) * p
        a = jnp.exp(inter_h - m_t)
        st_old = state_ref[hd]
        cq = _dot(st_old.astype(BF16), qt)
        num = a * cq[0:ML_V_DIM] + _dot(vt, sc.astype(BF16))
        den = a * cq[ML_V_DIM:ML_V_DIM + 1] + jnp.sum(sc, axis=0, keepdims=True)
        hc = num * (1.0 / jnp.maximum(jnp.abs(den), jnp.exp(-m_t)))

        w_row = w_fm[hd:hd + 1, :]
        n_rows = jnp.where(
            lax.broadcasted_iota(jnp.int32, (2 * V7X_SUBLANES, L), 0) == 0,
            w_row, 0.0)
        vw = jnp.concatenate([vt.astype(F32) * w_row, n_rows],
                             axis=0).astype(BF16)
        state_ref[hd] = decay[hd:hd + 1, 0:ML_QK_DIM] * st_old + _dot(vw, kh)

        hn = hc * lax.rsqrt(jnp.mean(hc * hc, axis=0, keepdims=True) + EPS)
        ng_h = _lane_tile(ng[hd * ML_V_DIM:(hd + 1) * ML_V_DIM, :], reps)
        y_ref[hd * ML_V_DIM:(hd + 1) * ML_V_DIM, :] = (
            hn * ng_h * jax.nn.sigmoid(ot)).astype(BF16)

    out = _dot_tn(y_ref[...], wout_ref[...])
    return x + (1.0 + mod[2:3]) * out


def _mlstm_layer(x, mods, layer, norm_g, w_fm, w_k, w_gt, gate_bt, ng_fm,
                 w_out, j):
    bsz, s, d = x.shape
    L = ML_SEQ_TILE
    tok = lambda b, i: (b, i, 0)
    per_j = lambda b, i: (j, 0, 0)
    return pl.pallas_call(
        _mlstm_kernel,
        grid=(bsz, s // L),
        in_specs=[
            pl.BlockSpec((None, L, d), tok),
            pl.BlockSpec((None, None, 6, d), lambda b, i: (layer, b, 0, 0)),
            pl.BlockSpec((None, 1, d), lambda b, i: (layer, 0, 0)),
            _resident((None, ML_FM_ROWS, d), per_j),
            _resident((None, d, ML_QK_W), per_j),
            _resident((None, ML_GATE_ROWS, d), per_j),
            pl.BlockSpec((None, ML_GATE_ROWS, V7X_LANES), per_j),
            _resident((None, ML_V_W, V7X_LANES), per_j),
            _resident((None, ML_V_W, d), per_j),
        ],
        out_specs=pl.BlockSpec((None, L, d), tok),
        out_shape=jax.ShapeDtypeStruct(x.shape, F32),
        scratch_shapes=[
            pltpu.VMEM((ML_HEADS, ML_STATE_ROWS, ML_QK_DIM), F32),
            pltpu.VMEM((V7X_SUBLANES, V7X_LANES), F32),
            pltpu.VMEM((L // ML_CHUNK, ML_V_W, ML_CHUNK), BF16),
        ],
        compiler_params=_compiler_params(),
        name=f"mlstm_layer{layer}",
    )(x, mods, norm_g, w_fm, w_k, w_gt, gate_bt, ng_fm, w_out)


def _mlstm_weights(ml_w_in, ml_b_i, ml_b_f, ml_norm_g):
    n_ml = ml_w_in.shape[0]
    q, k, v, o, gi, gf = jnp.split(
        ml_w_in, [ML_QK_W, 2 * ML_QK_W, 2 * ML_QK_W + ML_V_W,
                  2 * ML_QK_W + 2 * ML_V_W, 2 * ML_QK_W + 2 * ML_V_W + ML_HEADS],
        axis=-1)
    w_fm = jnp.concatenate([q, v, o], axis=-1).transpose(0, 2, 1).astype(BF16)
    w_k = k.astype(BF16)
    row_pad = V7X_SUBLANES - ML_HEADS
    pad_rows = lambda t: jnp.pad(t, ((0, 0), (0, row_pad), (0, 0)))
    w_gt = jnp.concatenate([pad_rows(gi.transpose(0, 2, 1)),
                            pad_rows(gf.transpose(0, 2, 1))], axis=1).astype(BF16)
    bt = jnp.concatenate([pad_rows(ml_b_i[:, :, None]),
                          pad_rows(ml_b_f[:, :, None])], axis=1)
    gate_bt = jnp.broadcast_to(bt, (n_ml, ML_GATE_ROWS, V7X_LANES))
    ng_fm = jnp.broadcast_to(ml_norm_g[:, :, None], (n_ml, ML_V_W, V7X_LANES))
    return w_fm, w_k, w_gt, gate_bt, ng_fm


def _pair_gate_value(t):
    lead = t.shape[:-1]
    t = t.reshape(lead + (2, D_FF // FF_CHUNK, FF_CHUNK))
    return jnp.swapaxes(t, -3, -2).reshape(lead + (2 * D_FF,))


def kernel(x, c, ada_w, ada_b, norm_mix_g, norm_ffn_g, sc_w_in, sc_conv_w, sc_w_out, ml_w_in, ml_b_i, ml_b_f, ml_norm_g, ml_w_out, ffn_w_up, ffn_conv_w, ffn_conv_b, ffn_w_down, final_norm_g):
    mods = _ada_modulation(c, ada_w, ada_b)

    sc_w_in_b = sc_w_in.astype(BF16)
    sc_w_out_b = sc_w_out.astype(BF16)
    ml_weights = _mlstm_weights(ml_w_in, ml_b_i, ml_b_f, ml_norm_g)
    ml_w_out_b = ml_w_out.astype(BF16)
    ffn_w_up_b = _pair_gate_value(ffn_w_up).astype(BF16)
    ffn_conv_w = _pair_gate_value(ffn_conv_w)
    ffn_conv_b = _pair_gate_value(ffn_conv_b)
    ffn_w_down_b = ffn_w_down.astype(BF16)
    norm_mix_g = norm_mix_g.reshape(DEPTH, 1, D_MODEL)
    norm_ffn_g = norm_ffn_g.reshape(DEPTH, 1, D_MODEL)

    for layer in range(DEPTH):
        j = layer // 2
        if layer % 2 == 0:
            x = _sconv_layer(x, mods, layer, norm_mix_g, sc_w_in_b, sc_conv_w,
                             sc_w_out_b, j)
        else:
            x = _mlstm_layer(x, mods, layer, norm_mix_g, *ml_weights,
                             ml_w_out_b, j)
        x = _ffn_layer(x, mods, layer, norm_ffn_g, ffn_w_up_b, ffn_conv_w,
                       ffn_conv_b, ffn_w_down_b, final_norm_g,
                       final_norm=(layer == DEPTH - 1))
    return x
```
